```python
import jax, jax.numpy as jnp
from jax import lax
import numpy as np

D_MODEL = 4096
BATCH = 2
SEQ = 8192
DEPTH = 1

N_META = 16
BLOCK = 128
PAD = BLOCK - N_META
EPS = 1e-6
NEG = -1e30
FOX_HEADS = 16
FOX_HD = 128
FOX_W = FOX_HEADS * FOX_HD
RET_HEADS = 16
RET_DK = 128
RET_DV = 256
RET_QK_W = RET_HEADS * RET_DK
RET_V_W = RET_HEADS * RET_DV
ROPE_BASE = 10000.0
PEER_HEADS = 8
PEER_NKEYS = 128
PEER_N = PEER_NKEYS * PEER_NKEYS
PEER_QDIM = 256
PEER_HALF = PEER_QDIM // 2
PEER_TOPK = 16
PEER_CHUNK = 128
W_IN_SPLITS = (FOX_W, FOX_W, FOX_W, FOX_HEADS,
               RET_QK_W, RET_QK_W, RET_V_W, RET_V_W,
               D_MODEL, D_MODEL)
W_IN_COLS = 3 * FOX_W + FOX_HEADS + 2 * RET_QK_W + 2 * RET_V_W + 2 * D_MODEL

kernel_name = "hybrid_fox_retnet_peer_block"


def _rmsnorm(x, g):
    xf = x.astype(jnp.float32)
    xf = xf * lax.rsqrt(jnp.mean(xf * xf, axis=-1, keepdims=True) + EPS)
    return xf.astype(x.dtype) * g


def _split_points():
    pts, acc = [], 0
    for w in W_IN_SPLITS[:-1]:
        acc += w
        pts.append(acc)
    return pts


def _heads(t, n_heads):
    b, l, _ = t.shape
    return t.reshape(b, l, n_heads, -1).transpose(0, 2, 1, 3)


def _rotary(x, pos):
    half = x.shape[-1] // 2
    inv = ROPE_BASE ** (-jnp.arange(half, dtype=jnp.float32) / half)
    ang = pos[:, None] * inv[None, :]
    cos, sin = jnp.cos(ang), jnp.sin(ang)
    xf = x.astype(jnp.float32)
    x1, x2 = xf[..., :half], xf[..., half:]
    return jnp.concatenate([x1 * cos - x2 * sin, x1 * sin + x2 * cos], axis=-1).astype(x.dtype)


def _fox_attention(q, k, v, logf):
    b, h, l, hd = q.shape
    nb = l // BLOCK
    c = jnp.cumsum(logf, axis=-1)
    kpos = jnp.arange(l)
    scale = hd ** -0.5

    def one_block(i):
        start = i * BLOCK
        qb = lax.dynamic_slice_in_dim(q, start, BLOCK, axis=2)
        cb = lax.dynamic_slice_in_dim(c, start, BLOCK, axis=2)
        qpos = start + jnp.arange(BLOCK)
        s = jnp.einsum('bhqd,bhkd->bhqk', qb, k).astype(jnp.float32) * scale
        s = s + (cb[..., :, None] - c[..., None, :])
        mask = (kpos[None, :] <= qpos[:, None]) & (kpos[None, :] >= PAD)
        s = jnp.where(mask, s, NEG)
        p = jax.nn.softmax(s, axis=-1)
        return jnp.einsum('bhqk,bhkd->bhqd', p.astype(v.dtype), v)

    out = lax.map(one_block, jnp.arange(nb))
    return out.transpose(1, 0, 3, 2, 4).reshape(b, l, h * hd)


def _retention(q, k, v):
    b, h, l, dk = q.shape
    dv = v.shape[-1]
    nc = l // BLOCK
    log_gamma = jnp.log1p(-(2.0 ** (-5.0 - jnp.arange(h, dtype=jnp.float32))))
    idx = jnp.arange(BLOCK, dtype=jnp.float32)
    diff = idx[:, None] - idx[None, :]
    decay = jnp.where(diff >= 0, jnp.exp(log_gamma[:, None, None] * jnp.maximum(diff, 0.0)), 0.0)
    xi = jnp.exp(log_gamma[:, None] * (idx + 1.0))
    zeta = jnp.exp(log_gamma[:, None] * (BLOCK - 1.0 - idx))
    chunk_decay = jnp.exp(log_gamma * BLOCK)

    qc = q.reshape(b, h, nc, BLOCK, dk).astype(jnp.float32)
    kc = k.reshape(b, h, nc, BLOCK, dk).astype(jnp.float32)
    vc = v.reshape(b, h, nc, BLOCK, dv).astype(jnp.float32)
    s = jnp.einsum('bhnid,bhnjd->bhnij', qc, kc) * decay[:, None]
    intra = jnp.einsum('bhnij,bhnje->bhnie', s, vc)

    def step(state, inp):
        qn, kn, vn = inp
        cross = jnp.einsum('bhid,bhde->bhie', qn, state) * xi[..., None]
        state = state * chunk_decay[:, None, None] + jnp.einsum('bhjd,bhje->bhde', kn * zeta[..., None], vn)
        return state, cross

    state0 = jnp.zeros((b, h, dk, dv), jnp.float32)
    _, cross = lax.scan(step, state0, (qc.transpose(2, 0, 1, 3, 4),
                                       kc.transpose(2, 0, 1, 3, 4),
                                       vc.transpose(2, 0, 1, 3, 4)))
    o = intra + cross.transpose(1, 2, 0, 3, 4)
    return o.reshape(b, h, l, dv)


def _head_groupnorm(o, g):
    mu = jnp.mean(o, axis=-1, keepdims=True)
    var = jnp.mean(jnp.square(o - mu), axis=-1, keepdims=True)
    o = (o - mu) * lax.rsqrt(var + EPS)
    b, h, l, dv = o.shape
    return o.transpose(0, 2, 1, 3).reshape(b, l, h * dv) * g.astype(jnp.float32)


def _mixer(h, pos, valid, norm_g, w_in, b_forget, q_norm_g, k_norm_g, ret_norm_g, w_proj_fox, w_proj_ret, w_out):
    u = _rmsnorm(h, norm_g)
    z = u @ w_in
    qa, ka, va, fa, qr, kr, vr, gr, gate_a, gate_r = jnp.split(z, _split_points(), axis=-1)
    qa = _rmsnorm(_heads(qa, FOX_HEADS), q_norm_g)
    ka = _rmsnorm(_heads(ka, FOX_HEADS), k_norm_g)
    logf = jax.nn.log_sigmoid((fa + b_forget).astype(jnp.float32)).transpose(0, 2, 1)
    ya = _fox_attention(qa, ka, _heads(va, FOX_HEADS), logf)
    vmask = valid.astype(h.dtype)[:, None]
    qr = _rotary(_heads(qr, RET_HEADS), pos)
    kr = _rotary(_heads(kr, RET_HEADS), pos) * (RET_DK ** -0.5) * vmask
    vr = _heads(vr, RET_HEADS) * vmask
    o = _head_groupnorm(_retention(qr, kr, vr), ret_norm_g)
    yr = jax.nn.silu(gr) * o.astype(h.dtype)
    merged = jax.nn.sigmoid(gate_a) * (ya @ w_proj_fox) + jax.nn.sigmoid(gate_r) * (yr @ w_proj_ret)
    return merged @ w_out


def _peer(h, norm_g, w_q, keys_1, keys_2, u_tab, v_tab):
    b, l, d = h.shape
    x = _rmsnorm(h, norm_g)
    tokens = x.reshape(-1, PEER_CHUNK, d)

    def chunk(t):
        q = (t @ w_q).reshape(PEER_CHUNK, PEER_HEADS, PEER_QDIM)
        q1, q2 = q[..., :PEER_HALF], q[..., PEER_HALF:]
        s1 = jnp.einsum('thd,hnd->thn', q1, keys_1)
        s2 = jnp.einsum('thd,hnd->thn', q2, keys_2)
        v1, i1 = lax.top_k(s1, PEER_TOPK)
        v2, i2 = lax.top_k(s2, PEER_TOPK)
        cand = (v1[..., :, None] + v2[..., None, :]).reshape(PEER_CHUNK, PEER_HEADS, PEER_TOPK * PEER_TOPK)
        cid = (i1[..., :, None] * PEER_NKEYS + i2[..., None, :]).reshape(PEER_CHUNK, PEER_HEADS, PEER_TOPK * PEER_TOPK)
        sv, sp = lax.top_k(cand, PEER_TOPK)
        eid = jnp.take_along_axis(cid, sp, axis=-1)
        g = jax.nn.softmax(sv.astype(jnp.float32), axis=-1)
        act = jax.nn.gelu(jnp.einsum('td,thkd->thk', t, u_tab[eid]).astype(jnp.float32), approximate=False)
        return jnp.einsum('thk,thkd->td', (g * act).astype(t.dtype), v_tab[eid])

    y = lax.map(chunk, tokens)
    return y.reshape(b, l, d)


def setup_inputs(seed: int = 0) -> dict:
    key = jax.random.key(seed)
    ks = jax.random.split(key, 20)
    f32 = jnp.float32
    nrm = lambda k, shape, s: jax.random.normal(k, shape, f32) * s
    return {
        "x": nrm(ks[0], (BATCH, SEQ, D_MODEL), 1.0),
        "meta_tokens": nrm(ks[1], (N_META, D_MODEL), 1.0),
        "norm_mix_g": 1.0 + nrm(ks[2], (DEPTH, D_MODEL), 0.02),
        "w_in": nrm(ks[3], (DEPTH, D_MODEL, W_IN_COLS), D_MODEL ** -0.5),
        "b_forget": jnp.linspace(1.0, 5.0, FOX_HEADS, dtype=f32)[None, :] + nrm(ks[4], (DEPTH, FOX_HEADS), 0.1),
        "q_norm_g": 1.0 + nrm(ks[5], (DEPTH, FOX_HD), 0.02),
        "k_norm_g": 1.0 + nrm(ks[6], (DEPTH, FOX_HD), 0.02),
        "ret_norm_g": 1.0 + nrm(ks[7], (DEPTH, RET_V_W), 0.02),
        "w_proj_fox": nrm(ks[8], (DEPTH, FOX_W, D_MODEL), FOX_W ** -0.5),
        "w_proj_ret": nrm(ks[9], (DEPTH, RET_V_W, D_MODEL), RET_V_W ** -0.5),
        "w_out": nrm(ks[10], (DEPTH, D_MODEL, D_MODEL), D_MODEL ** -0.5),
        "norm_ffn_g": 1.0 + nrm(ks[11], (DEPTH, D_MODEL), 0.02),
        "peer_w_q": nrm(ks[12], (DEPTH, D_MODEL, PEER_HEADS * PEER_QDIM), D_MODEL ** -0.5),
        "peer_keys_1": nrm(ks[13], (DEPTH, PEER_HEADS, PEER_NKEYS, PEER_HALF), PEER_HALF ** -0.5),
        "peer_keys_2": nrm(ks[14], (DEPTH, PEER_HEADS, PEER_NKEYS, PEER_HALF), PEER_HALF ** -0.5),
        "peer_u": nrm(ks[15], (DEPTH, PEER_N, D_MODEL), D_MODEL ** -0.5),
        "peer_v": nrm(ks[16], (DEPTH, PEER_N, D_MODEL), (PEER_HEADS * PEER_TOPK) ** -0.5),
    }


def reference(x, meta_tokens, norm_mix_g, w_in, b_forget, q_norm_g, k_norm_g, ret_norm_g,
              w_proj_fox, w_proj_ret, w_out, norm_ffn_g, peer_w_q, peer_keys_1, peer_keys_2,
              peer_u, peer_v):
    b = x.shape[0]
    pad = jnp.zeros((b, PAD, D_MODEL), x.dtype)
    meta = jnp.broadcast_to(meta_tokens.astype(x.dtype)[None], (b, N_META, D_MODEL))
    h = jnp.concatenate([pad, meta, x], axis=1)
    lp = h.shape[1]
    pos = jnp.arange(lp, dtype=jnp.float32) - PAD
    valid = jnp.arange(lp) >= PAD
    for i in range(DEPTH):
        h = h + _mixer(h, pos, valid, norm_mix_g[i], w_in[i], b_forget[i], q_norm_g[i], k_norm_g[i],
                       ret_norm_g[i], w_proj_fox[i], w_proj_ret[i], w_out[i])
        h = h + _peer(h, norm_ffn_g[i], peer_w_q[i], peer_keys_1[i], peer_keys_2[i], peer_u[i], peer_v[i])
    return h[:, PAD + N_META:]
```

```python
import functools

import jax
import jax.numpy as jnp
from jax import lax
from jax.experimental import pallas as pl
from jax.experimental.pallas import tpu as pltpu

N_META = 16
BLOCK = 128
PAD = BLOCK - N_META
EPS = 1e-6
NEG = -1e30
FOX_HEADS = 16
FOX_HD = 128
FOX_W = FOX_HEADS * FOX_HD
RET_HEADS = 16
RET_DK = 128
RET_DV = 256
RET_QK_W = RET_HEADS * RET_DK
RET_V_W = RET_HEADS * RET_DV
ROPE_BASE = 10000.0
PEER_HEADS = 8
PEER_NKEYS = 128
PEER_N = PEER_NKEYS * PEER_NKEYS
PEER_QDIM = 256
PEER_HALF = PEER_QDIM // 2
PEER_TOPK = 16

V7X_LANES = 128
V7X_VMEM_BYTES = 64 * 1024 * 1024
V7X_VMEM_BUDGET = 56 * 1024 * 1024

BF16 = jnp.bfloat16
F32 = jnp.float32

Z_FQ = 0
Z_FK = Z_FQ + FOX_W
Z_RQ = Z_FK + FOX_W
Z_RK = Z_RQ + RET_QK_W
Z_FV = Z_RK + RET_QK_W
Z_RV = Z_FV + FOX_W
Z_GR = Z_RV + RET_V_W
Z_GA = Z_GR + RET_V_W


def _params(sem, vmem=None):
    return pltpu.CompilerParams(dimension_semantics=sem, vmem_limit_bytes=vmem)


def _nt(a, b):
    return lax.dot_general(a, b, (((1,), (1,)), ((), ())), preferred_element_type=F32)


def _tn(a, b):
    return lax.dot_general(a, b, (((0,), (0,)), ((), ())), preferred_element_type=F32)


def _sigmoid(x):
    return 1.0 / (1.0 + jnp.exp(-x))


def _rmsnorm_kernel(x_ref, g_ref, o_ref):
    x = x_ref[...]
    ms = jnp.mean(x * x, axis=-1, keepdims=True)
    o_ref[...] = (x * lax.rsqrt(ms + EPS) * g_ref[...]).astype(o_ref.dtype)


def _rmsnorm_bf16(x, g, tm):
    m, d = x.shape
    return pl.pallas_call(
        _rmsnorm_kernel,
        grid=(m // tm,),
        in_specs=[pl.BlockSpec((tm, d), lambda i: (i, 0)),
                  pl.BlockSpec((1, d), lambda i: (0, 0))],
        out_specs=pl.BlockSpec((tm, d), lambda i: (i, 0)),
        out_shape=jax.ShapeDtypeStruct((m, d), BF16),
        compiler_params=_params(("parallel",), V7X_VMEM_BUDGET),
        name="rmsnorm_bf16",
    )(x, g.reshape(1, d))


def _proj_kernel(u_ref, w_ref, gain_ref, rc_ref, rs_ref, o_ref, *, classes, first_valid_row):
    j = pl.program_id(1)
    acc = jnp.dot(u_ref[...], w_ref[...], preferred_element_type=F32)
    tm, tn = acc.shape
    groups = [slice(g * V7X_LANES, (g + 1) * V7X_LANES) for g in range(tn // V7X_LANES)]

    def emit(kind):
        if kind == "headnorm":
            for sl in groups:
                x = acc[:, sl]
                ms = jnp.mean(x * x, axis=-1, keepdims=True)
                o_ref[:, sl] = (x * lax.rsqrt(ms + EPS) * gain_ref[:, sl]).astype(o_ref.dtype)
        elif kind == "rotary":
            c = rc_ref[...]
            s = rs_ref[...]
            for sl in groups:
                x = acc[:, sl]
                o_ref[:, sl] = (x * c + pltpu.roll(x, V7X_LANES // 2, 1) * s).astype(o_ref.dtype)
        elif kind == "plain":
            o_ref[...] = acc.astype(o_ref.dtype)
        elif kind == "vmasked":
            if first_valid_row > 0:
                rows = lax.broadcasted_iota(jnp.int32, (tm, 1), 0) + pl.program_id(0) * tm
                o_ref[...] = jnp.where(rows >= first_valid_row, acc, 0.0).astype(o_ref.dtype)
            else:
                o_ref[...] = acc.astype(o_ref.dtype)
        elif kind == "silu":
            o_ref[...] = (acc * _sigmoid(acc)).astype(o_ref.dtype)
        elif kind == "sigmoid":
            o_ref[...] = _sigmoid(acc).astype(o_ref.dtype)
        else:
            raise ValueError(kind)

    for kind, lo, hi in classes:
        pl.when((j >= lo) & (j < hi))(functools.partial(emit, kind))


def _proj(u, wp, gain, rot_c, rot_s, *, tm, tn, d_model, first_valid_row):
    m, d = u.shape
    npk = wp.shape[1]
    n_row_tiles = rot_c.shape[1] // tm
    widths = (("headnorm", 2 * FOX_W), ("rotary", 2 * RET_QK_W), ("plain", FOX_W),
              ("vmasked", RET_V_W), ("silu", RET_V_W), ("sigmoid", 2 * d_model))
    classes, lo = [], 0
    for kind, w in widths:
        classes.append((kind, lo, lo + w // tn))
        lo += w // tn
    assert lo * tn == npk
    rk_lo = Z_RK // tn
    kern = functools.partial(_proj_kernel, classes=tuple(classes), first_valid_row=first_valid_row)
    rot_map = lambda i, j: ((j >= rk_lo).astype(jnp.int32), i % n_row_tiles, 0)
    return pl.pallas_call(
        kern,
        grid=(m // tm, npk // tn),
        in_specs=[pl.BlockSpec((tm, d), lambda i, j: (i, 0)),
                  pl.BlockSpec((d, tn), lambda i, j: (0, j)),
                  pl.BlockSpec((1, tn), lambda i, j: (0, j)),
                  pl.BlockSpec((None, tm, V7X_LANES), rot_map),
                  pl.BlockSpec((None, tm, V7X_LANES), rot_map)],
        out_specs=pl.BlockSpec((tm, tn), lambda i, j: (i, j)),
        out_shape=jax.ShapeDtypeStruct((m, npk), BF16),
        compiler_params=_params(("parallel", "arbitrary"), V7X_VMEM_BUDGET),
        name="input_proj",
    )(u, wp, gain, rot_c, rot_s)


def _forget_kernel(u_ref, w_ref, b_ref, cin_ref, o_ref, carry_ref, *, first_valid_row):
    i = pl.program_id(1)

    @pl.when(i == 0)
    def _():
        carry_ref[...] = cin_ref[...]

    x = jnp.dot(u_ref[...], w_ref[...], preferred_element_type=F32) + b_ref[...]
    tm = x.shape[0]
    logf = jnp.minimum(x, 0.0) - jnp.log1p(jnp.exp(-jnp.abs(x)))
    rows = lax.broadcasted_iota(jnp.int32, (tm, tm), 0)
    cols = lax.broadcasted_iota(jnp.int32, (tm, tm), 1)
    if first_valid_row > 0:
        r1 = lax.broadcasted_iota(jnp.int32, (tm, 1), 0) + i * tm
        logf = jnp.where(r1 >= first_valid_row, logf, 0.0)
    tri = jnp.where(rows >= cols, 1.0, 0.0).astype(BF16)
    hi = logf.astype(BF16)
    r = logf - hi.astype(F32)
    mid = r.astype(BF16)
    lo = (r - mid.astype(F32)).astype(BF16)
    cs = (jnp.dot(tri, hi, preferred_element_type=F32)
          + jnp.dot(tri, mid, preferred_element_type=F32)
          + jnp.dot(tri, lo, preferred_element_type=F32))
    c = cs + carry_ref[...]
    o_ref[...] = c
    carry_ref[...] = c[tm - 1:tm, :]


def _forget_cumsum(u, wf, bf, carry_in, *, nb, tm, first_valid_row):
    m, d = u.shape
    nt = m // (nb * tm)
    kern = functools.partial(_forget_kernel, first_valid_row=first_valid_row)
    return pl.pallas_call(
        kern,
        grid=(nb, nt),
        in_specs=[pl.BlockSpec((tm, d), lambda b, i: (b * nt + i, 0)),
                  pl.BlockSpec((d, V7X_LANES), lambda b, i: (0, 0)),
                  pl.BlockSpec((1, V7X_LANES), lambda b, i: (0, 0)),
                  pl.BlockSpec((1, V7X_LANES), lambda b, i: (0, 0))],
        out_specs=pl.BlockSpec((tm, V7X_LANES), lambda b, i: (b * nt + i, 0)),
        out_shape=jax.ShapeDtypeStruct((m, V7X_LANES), F32),
        scratch_shapes=[pltpu.VMEM((1, V7X_LANES), F32)],
        compiler_params=_params(("parallel", "arbitrary"), V7X_VMEM_BUDGET),
        name="forget_cumsum",
    )(u, wf, bf, carry_in)


def _fox_kernel(qi_ref, kj_ref, kind_ref, q_ref, km_ref, vm_ref, k_ref, v_ref, cq_ref, cm_ref,
                ck_ref, o_ref, m_sc, l_sc, acc_sc):
    t = pl.program_id(2)
    kind = kind_ref[t]
    q = q_ref[...]
    c0 = cq_ref[:, 0:1]

    def update(s, v):
        m_prev = m_sc[...]
        m_new = jnp.maximum(m_prev, jnp.max(s, axis=-1, keepdims=True))
        alpha = jnp.exp(m_prev - m_new)
        p = jnp.exp(s - m_new)
        l_sc[...] = alpha * l_sc[...] + jnp.sum(p, axis=-1, keepdims=True)
        acc_sc[...] = alpha * acc_sc[...] + jnp.dot(p.astype(BF16), v, preferred_element_type=F32)
        m_sc[...] = m_new

    @pl.when(kind == 0)
    def _():
        s = _nt(q, km_ref[...]) + (c0 - cm_ref[...])
        kpos = lax.broadcasted_iota(jnp.int32, s.shape, 1)
        s = jnp.where(kpos >= PAD, s, NEG)
        m = jnp.max(s, axis=-1, keepdims=True)
        p = jnp.exp(s - m)
        m_sc[...] = m
        l_sc[...] = jnp.sum(p, axis=-1, keepdims=True)
        acc_sc[...] = jnp.dot(p.astype(BF16), vm_ref[...], preferred_element_type=F32)

    @pl.when(kind == 1)
    def _():
        s = _nt(q, k_ref[...]) + (c0 - ck_ref[...])
        update(s, v_ref[...])

    @pl.when(kind == 2)
    def _():
        s = _nt(q, k_ref[...]) + (c0 - ck_ref[...])
        rows = lax.broadcasted_iota(jnp.int32, s.shape, 0)
        cols = lax.broadcasted_iota(jnp.int32, s.shape, 1)
        s = jnp.where(cols <= rows, s, NEG)
        update(s, v_ref[...])
        o_ref[...] = (acc_sc[...] / l_sc[...]).astype(o_ref.dtype)


def _fox_attention(z3, zm, c_rows, cm_rows, *, tq):
    b, s, _ = z3.shape
    nq = s // tq
    qi, kj, kind = [], [], []
    for i in range(nq):
        qi += [i] * (i + 2)
        kj += [0] + list(range(i + 1))
        kind += [0] + [1] * i + [2]
    n_steps = len(qi)
    qi = jnp.asarray(qi, jnp.int32)
    kj = jnp.asarray(kj, jnp.int32)
    kind = jnp.asarray(kind, jnp.int32)
    fq, fk, fv = Z_FQ // FOX_HD, Z_FK // FOX_HD, Z_FV // FOX_HD
    grid_spec = pltpu.PrefetchScalarGridSpec(
        num_scalar_prefetch=3,
        grid=(b, FOX_HEADS, n_steps),
        in_specs=[
            pl.BlockSpec((None, tq, FOX_HD), lambda b_, h, t, qi, kj, kd: (b_, qi[t], fq + h)),
            pl.BlockSpec((BLOCK, FOX_HD), lambda b_, h, t, qi, kj, kd: (0, fk + h)),
            pl.BlockSpec((BLOCK, FOX_HD), lambda b_, h, t, qi, kj, kd: (0, fv + h)),
            pl.BlockSpec((None, tq, FOX_HD), lambda b_, h, t, qi, kj, kd: (b_, kj[t], fk + h)),
            pl.BlockSpec((None, tq, FOX_HD), lambda b_, h, t, qi, kj, kd: (b_, kj[t], fv + h)),
            pl.BlockSpec((None, None, 1, tq), lambda b_, h, t, qi, kj, kd: (b_, h, 0, qi[t])),
            pl.BlockSpec((None, 1, BLOCK), lambda b_, h, t, qi, kj, kd: (h, 0, 0)),
            pl.BlockSpec((None, None, 1, tq), lambda b_, h, t, qi, kj, kd: (b_, h, 0, kj[t])),
        ],
        out_specs=pl.BlockSpec((None, tq, FOX_HD), lambda b_, h, t, qi, kj, kd: (b_, qi[t], h)),
        scratch_shapes=[pltpu.VMEM((tq, 1), F32), pltpu.VMEM((tq, 1), F32),
                        pltpu.VMEM((tq, FOX_HD), F32)],
    )
    return pl.pallas_call(
        _fox_kernel,
        grid_spec=grid_spec,
        out_shape=jax.ShapeDtypeStruct((b, s, FOX_W), BF16),
        compiler_params=_params(("parallel", "parallel", "arbitrary")),
        name="fox_attention",
    )(qi, kj, kind, z3, zm, zm, z3, z3, c_rows, cm_rows, c_rows)


def _ret_kernel(q_ref, k_ref, v_ref, g_ref, km_ref, vm_ref, decay_ref, xi_ref, zeta_ref, zetam_ref,
                cd_ref, gn_ref, o_ref, state_sc):
    c = pl.program_id(2)

    @pl.when(c == 0)
    def _():
        kz = (km_ref[...].astype(F32) * zetam_ref[...]).astype(BF16)
        state_sc[...] = _tn(kz, vm_ref[...])

    q = q_ref[...]
    k = k_ref[...]
    v = v_ref[...]
    s = _nt(q, k) * decay_ref[...]
    intra = jnp.dot(s.astype(BF16), v, preferred_element_type=F32)
    state = state_sc[...]
    cross = jnp.dot(q, state.astype(BF16), preferred_element_type=F32) * xi_ref[...]
    o = intra + cross
    kz = (k.astype(F32) * zeta_ref[...]).astype(BF16)
    state_sc[...] = state * cd_ref[...] + _tn(kz, v)
    mu = jnp.mean(o, axis=-1, keepdims=True)
    d = o - mu
    var = jnp.mean(d * d, axis=-1, keepdims=True)
    y = d * lax.rsqrt(var + EPS) * gn_ref[...]
    o_ref[...] = (g_ref[...].astype(F32) * y).astype(o_ref.dtype)


def _retention(z3, zm, tabs, ret_norm_g, *, chunk):
    b, s, _ = z3.shape
    nc = s // chunk
    rq, rk = Z_RQ // RET_DK, Z_RK // RET_DK
    rv, gr = Z_RV // RET_DV, Z_GR // RET_DV
    decay, xi, zeta, zetam, cd = tabs
    return pl.pallas_call(
        _ret_kernel,
        grid=(b, RET_HEADS, nc),
        in_specs=[
            pl.BlockSpec((None, chunk, RET_DK), lambda b_, h, c: (b_, c, rq + h)),
            pl.BlockSpec((None, chunk, RET_DK), lambda b_, h, c: (b_, c, rk + h)),
            pl.BlockSpec((None, chunk, RET_DV), lambda b_, h, c: (b_, c, rv + h)),
            pl.BlockSpec((None, chunk, RET_DV), lambda b_, h, c: (b_, c, gr + h)),
            pl.BlockSpec((BLOCK, RET_DK), lambda b_, h, c: (0, rk + h)),
            pl.BlockSpec((BLOCK, RET_DV), lambda b_, h, c: (0, rv + h)),
            pl.BlockSpec((None, chunk, chunk), lambda b_, h, c: (h, 0, 0)),
            pl.BlockSpec((None, chunk, RET_DV), lambda b_, h, c: (h, 0, 0)),
            pl.BlockSpec((None, chunk, RET_DK), lambda b_, h, c: (h, 0, 0)),
            pl.BlockSpec((None, BLOCK, RET_DK), lambda b_, h, c: (h, 0, 0)),
            pl.BlockSpec((None, 1, RET_DV), lambda b_, h, c: (h, 0, 0)),
            pl.BlockSpec((1, RET_DV), lambda b_, h, c: (0, h)),
        ],
        out_specs=pl.BlockSpec((None, chunk, RET_DV), lambda b_, h, c: (b_, c, h)),
        out_shape=jax.ShapeDtypeStruct((b, s, RET_V_W), BF16),
        scratch_shapes=[pltpu.VMEM((RET_DK, RET_DV), F32)],
        compiler_params=_params(("parallel", "parallel", "arbitrary")),
        name="retention",
    )(z3, z3, z3, z3, zm, zm, decay, xi, zeta, zetam, cd, ret_norm_g.reshape(1, RET_V_W))


def _retention_tables(chunk):
    hh = jnp.arange(RET_HEADS, dtype=F32)
    log_gamma = jnp.log1p(-(2.0 ** (-5.0 - hh)))
    lg = log_gamma[:, None, None]

    def tabs(n):
        idx = jnp.arange(n, dtype=F32)
        diff = idx[:, None] - idx[None, :]
        decay = jnp.where(diff >= 0, jnp.exp(lg * jnp.maximum(diff, 0.0)), 0.0)
        xi = jnp.exp(log_gamma[:, None] * (idx + 1.0))
        zeta = jnp.exp(log_gamma[:, None] * (n - 1.0 - idx))
        return decay, xi, zeta

    decay, xi, zeta = tabs(chunk)
    _, _, zetam = tabs(BLOCK)
    cd = jnp.exp(log_gamma * chunk)
    return (decay,
            jnp.broadcast_to(xi[:, :, None], (RET_HEADS, chunk, RET_DV)),
            jnp.broadcast_to(zeta[:, :, None], (RET_HEADS, chunk, RET_DK)),
            jnp.broadcast_to(zetam[:, :, None], (RET_HEADS, BLOCK, RET_DK)),
            jnp.broadcast_to(cd[:, None, None], (RET_HEADS, 1, RET_DV)))


def _merge_kernel(ya_ref, yr_ref, wf_ref, wr_ref, ga_ref, gr_ref, o_ref):
    a = jnp.dot(ya_ref[...], wf_ref[...], preferred_element_type=F32)
    r = jnp.dot(yr_ref[...], wr_ref[...], preferred_element_type=F32)
    o_ref[...] = (ga_ref[...].astype(F32) * a + gr_ref[...].astype(F32) * r).astype(o_ref.dtype)


def _merge(ya, yr, wf, wr, z, *, tm, tn, d_model):
    m = ya.shape[0]
    ga0 = Z_GA // tn
    gr0 = (Z_GA + d_model) // tn
    return pl.pallas_call(
        _merge_kernel,
        grid=(m // tm, d_model // tn),
        in_specs=[pl.BlockSpec((tm, FOX_W), lambda i, j: (i, 0)),
                  pl.BlockSpec((tm, RET_V_W), lambda i, j: (i, 0)),
                  pl.BlockSpec((FOX_W, tn), lambda i, j: (0, j)),
                  pl.BlockSpec((RET_V_W, tn), lambda i, j: (0, j)),
                  pl.BlockSpec((tm, tn), lambda i, j: (i, ga0 + j)),
                  pl.BlockSpec((tm, tn), lambda i, j: (i, gr0 + j))],
        out_specs=pl.BlockSpec((tm, tn), lambda i, j: (i, j)),
        out_shape=jax.ShapeDtypeStruct((m, d_model), BF16),
        compiler_params=_params(("parallel", "arbitrary"), V7X_VMEM_BUDGET),
        name="merge_proj",
    )(ya, yr, wf, wr, z, z)


def _outproj_kernel(a_ref, w_ref, x_ref, o_ref):
    o_ref[...] = x_ref[...] + jnp.dot(a_ref[...], w_ref[...], preferred_element_type=F32)


def _outproj(a, w, x, *, tm, tn):
    m, k = a.shape
    n = w.shape[1]
    return pl.pallas_call(
        _outproj_kernel,
        grid=(m // tm, n // tn),
        in_specs=[pl.BlockSpec((tm, k), lambda i, j: (i, 0)),
                  pl.BlockSpec((k, tn), lambda i, j: (0, j)),
                  pl.BlockSpec((tm, tn), lambda i, j: (i, j))],
        out_specs=pl.BlockSpec((tm, tn), lambda i, j: (i, j)),
        out_shape=jax.ShapeDtypeStruct((m, n), F32),
        compiler_params=_params(("parallel", "arbitrary"), V7X_VMEM_BUDGET),
        name="out_proj",
    )(a, w, x)


def _peer_score_kernel(x_ref, w_ref, keys_ref, o_ref):
    acc = jnp.dot(x_ref[...], w_ref[...], preferred_element_type=F32)
    for g in range(acc.shape[1] // PEER_HALF):
        qg = acc[:, g * PEER_HALF:(g + 1) * PEER_HALF].astype(BF16)
        o_ref[g] = _nt(keys_ref[g], qg)


def _peer_scores(xn, wq, keys, *, tm, tn):
    m, d = xn.shape
    n = wq.shape[1]
    ng = tn // PEER_HALF
    return pl.pallas_call(
        _peer_score_kernel,
        grid=(m // tm, n // tn),
        in_specs=[pl.BlockSpec((tm, d), lambda i, j: (i, 0)),
                  pl.BlockSpec((d, tn), lambda i, j: (0, j)),
                  pl.BlockSpec((ng, PEER_NKEYS, PEER_HALF), lambda i, j: (j, 0, 0))],
        out_specs=pl.BlockSpec((ng, PEER_NKEYS, tm), lambda i, j: (j, 0, i)),
        out_shape=jax.ShapeDtypeStruct((n // PEER_HALF, PEER_NKEYS, m), F32),
        compiler_params=_params(("parallel", "arbitrary"), V7X_VMEM_BUDGET),
        name="peer_scores",
    )(xn, wq, keys)


def _top_values(x, n, stacked_rows):
    rows, width = x.shape
    idx = lax.broadcasted_iota(jnp.int32, (rows, width), 0).astype(F32)
    srow = lax.broadcasted_iota(jnp.int32, (stacked_rows, width), 0)
    stacked = jnp.zeros((stacked_rows, width), F32)
    vals = []
    for t in range(n):
        cur = jnp.max(x, axis=0, keepdims=True)
        vals.append(cur)
        stacked = jnp.where(srow == t, cur, stacked)
        first = jnp.min(jnp.where(x == cur, idx, float(rows)), axis=0, keepdims=True)
        x = jnp.where(idx == first, -jnp.inf, x)
    return stacked, vals


def _peer_topk_kernel(s_ref, e1_ref, e2_ref, tau_ref):
    s1 = s_ref[0]
    s2 = s_ref[1]
    v1, v1l = _top_values(s1, PEER_TOPK, PEER_TOPK)
    v2, v2l = _top_values(s2, PEER_TOPK, PEER_TOPK)
    sub = lax.broadcasted_iota(jnp.int32, (8, s1.shape[1]), 0)
    pieces = [v1l[0] + v2]
    for a in range(1, 8):
        nb = PEER_TOPK // (a + 1)
        pieces.append(jnp.where(sub < nb, v1l[a] + v2[0:8], -jnp.inf))
    pieces.append(v1[8:16] + v2l[0])
    cand = jnp.concatenate(pieces, axis=0)
    _, top = _top_values(cand, PEER_TOPK, 8)
    m = top[0]
    z = jnp.zeros_like(m)
    for t in range(PEER_TOPK):
        z = z + jnp.exp(top[t] - m)
    tau_ref[...] = top[PEER_TOPK - 1]
    e1_ref[...] = jnp.exp(s1 - v1l[0])
    e2_ref[...] = jnp.exp(s2 - v2l[0]) / z


def _peer_topk(st, *, tt):
    _, _, t = st.shape
    return pl.pallas_call(
        _peer_topk_kernel,
        grid=(t // tt, PEER_HEADS),
        in_specs=[pl.BlockSpec((2, PEER_NKEYS, tt), lambda i, h: (h, 0, i))],
        out_specs=[pl.BlockSpec((None, PEER_NKEYS, tt), lambda i, h: (h, 0, i)),
                   pl.BlockSpec((None, PEER_NKEYS, tt), lambda i, h: (h, 0, i)),
                   pl.BlockSpec((None, 1, tt), lambda i, h: (h, 0, i))],
        out_shape=[jax.ShapeDtypeStruct((PEER_HEADS, PEER_NKEYS, t), F32),
                   jax.ShapeDtypeStruct((PEER_HEADS, PEER_NKEYS, t), F32),
                   jax.ShapeDtypeStruct((PEER_HEADS, 1, t), F32)],
        compiler_params=_params(("parallel", "parallel")),
        name="peer_topk",
    )(st)


def _gelu_exact(x):
    return 0.5 * x * (1.0 + lax.erf(x * (2.0 ** -0.5)))


def _peer_dense_kernel(x_ref, u_ref, v_ref, s_ref, e1_ref, e2_ref, tau_ref, o_ref, w_sc, *, na):
    j = pl.program_id(1)

    @pl.when(j == 0)
    def _():
        o_ref[...] = jnp.zeros_like(o_ref)

    act = _gelu_exact(_nt(u_ref[...], x_ref[...]))
    for a in range(na):
        row = j * na + a
        gate = None
        for h in range(PEER_HEADS):
            s1 = s_ref[2 * h, pl.ds(row, 1), :]
            e1 = e1_ref[h, pl.ds(row, 1), :]
            sel = (s1 + s_ref[2 * h + 1]) >= tau_ref[h]
            term = jnp.where(sel, e1 * e2_ref[h], 0.0)
            gate = term if gate is None else gate + term
        sl = slice(a * PEER_NKEYS, (a + 1) * PEER_NKEYS)
        w_sc[sl, :] = (gate * act[sl, :]).astype(BF16)
    o_ref[...] += _tn(w_sc[...], v_ref[...])


def _peer_dense(xn, u_tab, v_tab, st, e1, e2, tau, *, tt, ec):
    t, d = xn.shape
    n_exp = u_tab.shape[0]
    na = ec // PEER_NKEYS
    once = pl.Buffered(1)
    kern = functools.partial(_peer_dense_kernel, na=na)
    return pl.pallas_call(
        kern,
        grid=(t // tt, n_exp // ec),
        in_specs=[pl.BlockSpec((tt, d), lambda i, j: (i, 0), pipeline_mode=once),
                  pl.BlockSpec((ec, d), lambda i, j: (j, 0)),
                  pl.BlockSpec((ec, d), lambda i, j: (j, 0)),
                  pl.BlockSpec((2 * PEER_HEADS, PEER_NKEYS, tt), lambda i, j: (0, 0, i), pipeline_mode=once),
                  pl.BlockSpec((PEER_HEADS, PEER_NKEYS, tt), lambda i, j: (0, 0, i), pipeline_mode=once),
                  pl.BlockSpec((PEER_HEADS, PEER_NKEYS, tt), lambda i, j: (0, 0, i), pipeline_mode=once),
                  pl.BlockSpec((PEER_HEADS, 1, tt), lambda i, j: (0, 0, i), pipeline_mode=once)],
        out_specs=pl.BlockSpec((tt, d), lambda i, j: (i, 0)),
        out_shape=jax.ShapeDtypeStruct((t, d), F32),
        scratch_shapes=[pltpu.VMEM((ec, tt), BF16)],
        compiler_params=_params(("parallel", "arbitrary"), V7X_VMEM_BUDGET),
        name="peer_dense",
    )(xn, u_tab, v_tab, st, e1, e2, tau)


def _add_kernel(a_ref, b_ref, o_ref):
    o_ref[...] = a_ref[...] + b_ref[...]


def _add(a, b, tm):
    m, d = a.shape
    spec = pl.BlockSpec((tm, d), lambda i: (i, 0))
    return pl.pallas_call(
        _add_kernel, grid=(m // tm,), in_specs=[spec, spec], out_specs=spec,
        out_shape=jax.ShapeDtypeStruct((m, d), a.dtype),
        compiler_params=_params(("parallel",), V7X_VMEM_BUDGET), name="residual_add",
    )(a, b)


def _rotary_tables(pos, valid):
    half = RET_DK // 2
    inv = ROPE_BASE ** (-jnp.arange(half, dtype=F32) / half)
    ang = pos[:, None] * inv[None, :]
    cos, sin = jnp.cos(ang), jnp.sin(ang)
    c = jnp.concatenate([cos, cos], axis=-1)
    s = jnp.concatenate([-sin, sin], axis=-1)
    kscale = (RET_DK ** -0.5) * valid.astype(F32)[:, None]
    return jnp.stack([c, c * kscale]), jnp.stack([s, s * kscale])


def _tile(n, pref):
    t = min(n, pref)
    assert n % t == 0
    return t


def kernel(x, meta_tokens, norm_mix_g, w_in, b_forget, q_norm_g, k_norm_g, ret_norm_g, w_proj_fox,
           w_proj_ret, w_out, norm_ffn_g, peer_w_q, peer_keys_1, peer_keys_2, peer_u, peer_v):
    b, s, d = x.shape
    depth = w_in.shape[0]
    assert depth == 1, "only the real-token outputs of a single layer are computed"
    m = b * s
    xr = x.reshape(m, d)
    hm = jnp.concatenate([jnp.zeros((PAD, d), x.dtype), meta_tokens.astype(x.dtype)], axis=0)

    w = w_in[0]
    o_fq, o_fk, o_fv, o_f = 0, FOX_W, 2 * FOX_W, 3 * FOX_W
    o_rq = o_f + FOX_HEADS
    o_rk, o_rv = o_rq + RET_QK_W, o_rq + 2 * RET_QK_W
    o_gr = o_rv + RET_V_W
    o_ga = o_gr + RET_V_W
    o_gb = o_ga + d
    col = lambda o, n: w[:, o:o + n]
    wp = jnp.concatenate([col(o_fq, FOX_W), col(o_fk, FOX_W), col(o_rq, RET_QK_W), col(o_rk, RET_QK_W),
                          col(o_fv, FOX_W), col(o_rv, RET_V_W), col(o_gr, RET_V_W), col(o_ga, d),
                          col(o_gb, d)], axis=1).astype(BF16)
    npk = wp.shape[1]
    wf = jnp.pad(col(o_f, FOX_HEADS), ((0, 0), (0, V7X_LANES - FOX_HEADS))).astype(BF16)
    bf = jnp.pad(b_forget[0], (0, V7X_LANES - FOX_HEADS)).reshape(1, V7X_LANES)
    gain = jnp.concatenate([jnp.tile(q_norm_g[0] * (FOX_HD ** -0.5), FOX_HEADS),
                            jnp.tile(k_norm_g[0], FOX_HEADS),
                            jnp.zeros((npk - 2 * FOX_W,), F32)]).reshape(1, npk)

    pos_real = jnp.arange(s, dtype=F32) + N_META
    rc, rs = _rotary_tables(pos_real, jnp.ones((s,), bool))
    pos_meta = jnp.arange(BLOCK, dtype=F32) - PAD
    rcm, rsm = _rotary_tables(pos_meta, jnp.arange(BLOCK) >= PAD)

    tm = _tile(m, 1024)
    tn = _tile(d, 1024)

    u = _rmsnorm_bf16(xr, norm_mix_g[0], _tile(m, 512))
    um = _rmsnorm_bf16(hm, norm_mix_g[0], BLOCK)
    z = _proj(u, wp, gain, rc, rs, tm=_tile(s, 1024), tn=tn, d_model=d, first_valid_row=0)
    zm = _proj(um, wp, gain, rcm, rsm, tm=BLOCK, tn=tn, d_model=d, first_valid_row=PAD)
    cm = _forget_cumsum(um, wf, bf, jnp.zeros((1, V7X_LANES), F32), nb=1, tm=BLOCK, first_valid_row=PAD)
    c = _forget_cumsum(u, wf, bf, cm[BLOCK - 1:BLOCK], nb=b, tm=_tile(s, 512), first_valid_row=0)
    c_rows = c.reshape(b, s, V7X_LANES)[:, :, :FOX_HEADS].transpose(0, 2, 1).reshape(b, FOX_HEADS, 1, s)
    cm_rows = cm[:, :FOX_HEADS].T.reshape(FOX_HEADS, 1, BLOCK)

    z3 = z.reshape(b, s, npk)
    ya = _fox_attention(z3, zm, c_rows, cm_rows, tq=_tile(s, 512))
    chunk = _tile(s, 256)
    yr = _retention(z3, zm, _retention_tables(chunk), ret_norm_g[0], chunk=chunk)

    merged = _merge(ya.reshape(m, FOX_W), yr.reshape(m, RET_V_W), w_proj_fox[0].astype(BF16),
                    w_proj_ret[0].astype(BF16), z, tm=tm, tn=_tile(d, 512), d_model=d)
    h1 = _outproj(merged, w_out[0].astype(BF16), xr, tm=tm, tn=tn)

    xn = _rmsnorm_bf16(h1, norm_ffn_g[0], _tile(m, 512))
    keys = jnp.stack([peer_keys_1[0], peer_keys_2[0]], axis=1).reshape(
        2 * PEER_HEADS, PEER_NKEYS, PEER_HALF).astype(BF16)
    st = _peer_scores(xn, peer_w_q[0].astype(BF16), keys, tm=tm, tn=_tile(PEER_HEADS * PEER_QDIM, 1024))
    e1, e2, tau = _peer_topk(st, tt=_tile(m, 512))
    y = _peer_dense(xn, peer_u[0].astype(BF16), peer_v[0].astype(BF16), st, e1, e2, tau,
                    tt=_tile(m, 512), ec=512)
    return _add(h1, y, _tile(m, 512)).reshape(b, s, d)
```

```python
import functools

import jax
import jax.numpy as jnp
from jax import lax
from jax.experimental import pallas as pl
from jax.experimental.pallas import tpu as pltpu

N_META = 16
BLOCK = 128
PAD = BLOCK - N_META
EPS = 1e-6
NEG = -1e30
FOX_HEADS = 16
FOX_HD = 128
FOX_W = FOX_HEADS * FOX_HD
RET_HEADS = 16
RET_DK = 128
RET_DV = 256
RET_QK_W = RET_HEADS * RET_DK
RET_V_W = RET_HEADS * RET_DV
ROPE_BASE = 10000.0
PEER_HEADS = 8
PEER_NKEYS = 128
PEER_N = PEER_NKEYS * PEER_NKEYS
PEER_QDIM = 256
PEER_HALF = PEER_QDIM // 2
PEER_TOPK = 16

V7X_LANES = 128
V7X_VMEM_BYTES = 64 * 1024 * 1024
V7X_VMEM_BUDGET = 56 * 1024 * 1024

BF16 = jnp.bfloat16
F32 = jnp.float32

Z_FQ = 0
Z_FK = Z_FQ + FOX_W
Z_RQ = Z_FK + FOX_W
Z_RK = Z_RQ + RET_QK_W
Z_FV = Z_RK + RET_QK_W
Z_RV = Z_FV + FOX_W
Z_GR = Z_RV + RET_V_W
Z_GA = Z_GR + RET_V_W


def _params(sem, vmem=None):
    return pltpu.CompilerParams(dimension_semantics=sem, vmem_limit_bytes=vmem)


def _nt(a, b):
    return lax.dot_general(a, b, (((1,), (1,)), ((), ())), preferred_element_type=F32)


def _tn(a, b):
    return lax.dot_general(a, b, (((0,), (0,)), ((), ())), preferred_element_type=F32)


def _sigmoid(x):
    return 1.0 / (1.0 + jnp.exp(-x))


def _rmsnorm_kernel(x_ref, g_ref, o_ref):
    x = x_ref[...]
    ms = jnp.mean(x * x, axis=-1, keepdims=True)
    o_ref[...] = (x * lax.rsqrt(ms + EPS) * g_ref[...]).astype(o_ref.dtype)


def _rmsnorm_bf16(x, g, tm):
    m, d = x.shape
    return pl.pallas_call(
        _rmsnorm_kernel,
        grid=(m // tm,),
        in_specs=[pl.BlockSpec((tm, d), lambda i: (i, 0)),
                  pl.BlockSpec((1, d), lambda i: (0, 0))],
        out_specs=pl.BlockSpec((tm, d), lambda i: (i, 0)),
        out_shape=jax.ShapeDtypeStruct((m, d), BF16),
        compiler_params=_params(("parallel",), V7X_VMEM_BUDGET),
        name="rmsnorm_bf16",
    )(x, g.reshape(1, d))


def _proj_kernel(u_ref, w_ref, gain_ref, rc_ref, rs_ref, o_ref, *, classes, first_valid_row):
    j = pl.program_id(1)
    acc = jnp.dot(u_ref[...], w_ref[...], preferred_element_type=F32)
    tm, tn = acc.shape
    groups = [slice(g * V7X_LANES, (g + 1) * V7X_LANES) for g in range(tn // V7X_LANES)]

    def emit(kind):
        if kind == "headnorm":
            for sl in groups:
                x = acc[:, sl]
                ms = jnp.mean(x * x, axis=-1, keepdims=True)
                o_ref[:, sl] = (x * lax.rsqrt(ms + EPS) * gain_ref[:, sl]).astype(o_ref.dtype)
        elif kind == "rotary":
            c = rc_ref[...]
            s = rs_ref[...]
            for sl in groups:
                x = acc[:, sl]
                o_ref[:, sl] = (x * c + pltpu.roll(x, V7X_LANES // 2, 1) * s).astype(o_ref.dtype)
        elif kind == "plain":
            o_ref[...] = acc.astype(o_ref.dtype)
        elif kind == "vmasked":
            if first_valid_row > 0:
                rows = lax.broadcasted_iota(jnp.int32, (tm, 1), 0) + pl.program_id(0) * tm
                o_ref[...] = jnp.where(rows >= first_valid_row, acc, 0.0).astype(o_ref.dtype)
            else:
                o_ref[...] = acc.astype(o_ref.dtype)
        elif kind == "silu":
            o_ref[...] = (acc * _sigmoid(acc)).astype(o_ref.dtype)
        elif kind == "sigmoid":
            o_ref[...] = _sigmoid(acc).astype(o_ref.dtype)
        else:
            raise ValueError(kind)

    for kind, lo, hi in classes:
        pl.when((j >= lo) & (j < hi))(functools.partial(emit, kind))


def _proj(u, wp, gain, rot_c, rot_s, *, tm, tn, d_model, first_valid_row):
    m, d = u.shape
    npk = wp.shape[1]
    n_row_tiles = rot_c.shape[1] // tm
    widths = (("headnorm", 2 * FOX_W), ("rotary", 2 * RET_QK_W), ("plain", FOX_W),
              ("vmasked", RET_V_W), ("silu", RET_V_W), ("sigmoid", 2 * d_model))
    classes, lo = [], 0
    for kind, w in widths:
        classes.append((kind, lo, lo + w // tn))
        lo += w // tn
    assert lo * tn == npk
    rk_lo = Z_RK // tn
    kern = functools.partial(_proj_kernel, classes=tuple(classes), first_valid_row=first_valid_row)
    rot_map = lambda i, j: ((j >= rk_lo).astype(jnp.int32), i % n_row_tiles, 0)
    return pl.pallas_call(
        kern,
        grid=(m // tm, npk // tn),
        in_specs=[pl.BlockSpec((tm, d), lambda i, j: (i, 0)),
                  pl.BlockSpec((d, tn), lambda i, j: (0, j)),
                  pl.BlockSpec((1, tn), lambda i, j: (0, j)),
                  pl.BlockSpec((None, tm, V7X_LANES), rot_map),
                  pl.BlockSpec((None, tm, V7X_LANES), rot_map)],
        out_specs=pl.BlockSpec((tm, tn), lambda i, j: (i, j)),
        out_shape=jax.ShapeDtypeStruct((m, npk), BF16),
        compiler_params=_params(("parallel", "arbitrary"), V7X_VMEM_BUDGET),
        name="input_proj",
    )(u, wp, gain, rot_c, rot_s)


def _forget_kernel(u_ref, w_ref, b_ref, cin_ref, o_ref, carry_ref, *, first_valid_row):
    i = pl.program_id(1)

    @pl.when(i == 0)
    def _():
        carry_ref[...] = cin_ref[...]

    x = jnp.dot(u_ref[...], w_ref[...], preferred_element_type=F32) + b_ref[...]
    tm = x.shape[0]
    logf = jnp.minimum(x, 0.0) - jnp.log1p(jnp.exp(-jnp.abs(x)))
    rows = lax.broadcasted_iota(jnp.int32, (tm, tm), 0)
    cols = lax.broadcasted_iota(jnp.int32, (tm, tm), 1)
    if first_valid_row > 0:
        r1 = lax.broadcasted_iota(jnp.int32, (tm, 1), 0) + i * tm
        logf = jnp.where(r1 >= first_valid_row, logf, 0.0)
    tri = jnp.where(rows >= cols, 1.0, 0.0).astype(BF16)
    hi = logf.astype(BF16)
    r = logf - hi.astype(F32)
    mid = r.astype(BF16)
    lo = (r - mid.astype(F32)).astype(BF16)
    cs = (jnp.dot(tri, hi, preferred_element_type=F32)
          + jnp.dot(tri, mid, preferred_element_type=F32)
          + jnp.dot(tri, lo, preferred_element_type=F32))
    c = cs + carry_ref[...]
    o_ref[...] = c
    carry_ref[...] = c[tm - 1:tm, :]


def _forget_cumsum(u, wf, bf, carry_in, *, nb, tm, first_valid_row):
    m, d = u.shape
    nt = m // (nb * tm)
    kern = functools.partial(_forget_kernel, first_valid_row=first_valid_row)
    return pl.pallas_call(
        kern,
        grid=(nb, nt),
        in_specs=[pl.BlockSpec((tm, d), lambda b, i: (b * nt + i, 0)),
                  pl.BlockSpec((d, V7X_LANES), lambda b, i: (0, 0)),
                  pl.BlockSpec((1, V7X_LANES), lambda b, i: (0, 0)),
                  pl.BlockSpec((1, V7X_LANES), lambda b, i: (0, 0))],
        out_specs=pl.BlockSpec((tm, V7X_LANES), lambda b, i: (b * nt + i, 0)),
        out_shape=jax.ShapeDtypeStruct((m, V7X_LANES), F32),
        scratch_shapes=[pltpu.VMEM((1, V7X_LANES), F32)],
        compiler_params=_params(("parallel", "arbitrary"), V7X_VMEM_BUDGET),
        name="forget_cumsum",
    )(u, wf, bf, carry_in)


def _fox_kernel(qi_ref, kj_ref, kind_ref, q_ref, km_ref, vm_ref, k_ref, v_ref, cq_ref, cm_ref,
                ck_ref, o_ref, m_sc, acc_sc, *, heads):
    t = pl.program_id(2)
    kind = kind_ref[t]
    hd = FOX_HD
    lanes = [slice(g * hd, (g + 1) * hd) for g in range(heads)]

    def v_ext(v):
        return jnp.concatenate([v, jnp.ones(v.shape, v.dtype)], axis=1)

    def update(g, s, v):
        m_prev = m_sc[g]
        m_new = jnp.maximum(m_prev, jnp.max(s, axis=-1, keepdims=True))
        alpha = jnp.exp(m_prev - m_new)
        p = jnp.exp(s - m_new).astype(BF16)
        acc_sc[g] = alpha * acc_sc[g] + jnp.dot(p, v_ext(v), preferred_element_type=F32)
        m_sc[g] = m_new

    @pl.when(kind == 0)
    def _():
        for g, sl in enumerate(lanes):
            s = _nt(q_ref[:, sl], km_ref[:, sl]) + (cq_ref[g][:, 0:1] - cm_ref[g])
            kpos = lax.broadcasted_iota(jnp.int32, s.shape, 1)
            s = jnp.where(kpos >= PAD, s, NEG)
            m = jnp.max(s, axis=-1, keepdims=True)
            p = jnp.exp(s - m).astype(BF16)
            m_sc[g] = m
            acc_sc[g] = jnp.dot(p, v_ext(vm_ref[:, sl]), preferred_element_type=F32)

    @pl.when(kind == 1)
    def _():
        for g, sl in enumerate(lanes):
            s = _nt(q_ref[:, sl], k_ref[:, sl]) + (cq_ref[g][:, 0:1] - ck_ref[g])
            update(g, s, v_ref[:, sl])

    @pl.when(kind == 2)
    def _():
        for g, sl in enumerate(lanes):
            s = _nt(q_ref[:, sl], k_ref[:, sl]) + (cq_ref[g][:, 0:1] - ck_ref[g])
            rows = lax.broadcasted_iota(jnp.int32, s.shape, 0)
            cols = lax.broadcasted_iota(jnp.int32, s.shape, 1)
            s = jnp.where(cols <= rows, s, NEG)
            update(g, s, v_ref[:, sl])
            acc = acc_sc[g]
            o_ref[:, sl] = (acc[:, :hd] / acc[:, hd:]).astype(o_ref.dtype)


def _fox_attention(z3, zm, c_rows, cm_rows, *, tq, heads):
    b, s, _ = z3.shape
    nq = s // tq
    qi, kj, kind = [], [], []
    for i in range(nq):
        qi += [i] * (i + 2)
        kj += [0] + list(range(i + 1))
        kind += [0] + [1] * i + [2]
    n_steps = len(qi)
    qi = jnp.asarray(qi, jnp.int32)
    kj = jnp.asarray(kj, jnp.int32)
    kind = jnp.asarray(kind, jnp.int32)
    w = heads * FOX_HD
    fq, fk, fv = Z_FQ // w, Z_FK // w, Z_FV // w
    grid_spec = pltpu.PrefetchScalarGridSpec(
        num_scalar_prefetch=3,
        grid=(b, FOX_HEADS // heads, n_steps),
        in_specs=[
            pl.BlockSpec((None, tq, w), lambda b_, h, t, qi, kj, kd: (b_, qi[t], fq + h)),
            pl.BlockSpec((BLOCK, w), lambda b_, h, t, qi, kj, kd: (0, fk + h)),
            pl.BlockSpec((BLOCK, w), lambda b_, h, t, qi, kj, kd: (0, fv + h)),
            pl.BlockSpec((None, tq, w), lambda b_, h, t, qi, kj, kd: (b_, kj[t], fk + h)),
            pl.BlockSpec((None, tq, w), lambda b_, h, t, qi, kj, kd: (b_, kj[t], fv + h)),
            pl.BlockSpec((None, heads, 1, tq), lambda b_, h, t, qi, kj, kd: (b_, h, 0, qi[t])),
            pl.BlockSpec((heads, 1, BLOCK), lambda b_, h, t, qi, kj, kd: (h, 0, 0)),
            pl.BlockSpec((None, heads, 1, tq), lambda b_, h, t, qi, kj, kd: (b_, h, 0, kj[t])),
        ],
        out_specs=pl.BlockSpec((None, tq, w), lambda b_, h, t, qi, kj, kd: (b_, qi[t], h)),
        scratch_shapes=[pltpu.VMEM((heads, tq, 1), F32),
                        pltpu.VMEM((heads, tq, 2 * FOX_HD), F32)],
    )
    return pl.pallas_call(
        functools.partial(_fox_kernel, heads=heads),
        grid_spec=grid_spec,
        out_shape=jax.ShapeDtypeStruct((b, s, FOX_W), BF16),
        compiler_params=_params(("parallel", "parallel", "arbitrary")),
        name="fox_attention",
    )(qi, kj, kind, z3, zm, zm, z3, z3, c_rows, cm_rows, c_rows)


def _ret_kernel(q_ref, k_ref, v_ref, g_ref, km_ref, vm_ref, gn_ref, o_ref, state_sc, decay_sc,
                xi_sc, zeta_sc, cd_sc, *, heads):
    hg = pl.program_id(1)
    c = pl.program_id(2)
    chunk = q_ref.shape[0]
    dk, dv = RET_DK, RET_DV

    def log_gamma(head, width):
        hv = jnp.full((1, width), head, jnp.int32).astype(F32)
        return jnp.log1p(-jnp.exp2(-5.0 - hv))

    def rows(n, width):
        return lax.broadcasted_iota(jnp.int32, (n, width), 0).astype(F32)

    @pl.when(c == 0)
    def _():
        for g in range(heads):
            head = hg * heads + g
            diff = rows(chunk, chunk) - lax.broadcasted_iota(jnp.int32, (chunk, chunk), 1).astype(F32)
            decay_sc[g] = jnp.where(diff >= 0, jnp.exp(log_gamma(head, chunk) * jnp.maximum(diff, 0.0)), 0.0)
            xi_sc[g] = jnp.exp(log_gamma(head, dv) * (rows(chunk, dv) + 1.0))
            zeta_sc[g] = jnp.exp(log_gamma(head, dk) * (chunk - 1.0 - rows(chunk, dk)))
            cd_sc[g] = jnp.exp(log_gamma(head, dv) * float(chunk))
            zeta_m = jnp.exp(log_gamma(head, dk) * (BLOCK - 1.0 - rows(BLOCK, dk)))
            kz = (km_ref[:, g * dk:(g + 1) * dk].astype(F32) * zeta_m).astype(BF16)
            state_sc[g] = _tn(kz, vm_ref[:, g * dv:(g + 1) * dv])

    for g in range(heads):
        q = q_ref[:, g * dk:(g + 1) * dk]
        k = k_ref[:, g * dk:(g + 1) * dk]
        v = v_ref[:, g * dv:(g + 1) * dv]
        s = _nt(q, k) * decay_sc[g]
        intra = jnp.dot(s.astype(BF16), v, preferred_element_type=F32)
        state = state_sc[g]
        cross = jnp.dot(q, state.astype(BF16), preferred_element_type=F32) * xi_sc[g]
        o = intra + cross
        kz = (k.astype(F32) * zeta_sc[g]).astype(BF16)
        state_sc[g] = state * cd_sc[g] + _tn(kz, v)
        mu = jnp.mean(o, axis=-1, keepdims=True)
        d = o - mu
        var = jnp.mean(d * d, axis=-1, keepdims=True)
        y = d * lax.rsqrt(var + EPS) * gn_ref[:, g * dv:(g + 1) * dv]
        o_ref[:, g * dv:(g + 1) * dv] = (g_ref[:, g * dv:(g + 1) * dv].astype(F32) * y).astype(o_ref.dtype)


def _retention(z3, zm, ret_norm_g, *, chunk, heads):
    b, s, _ = z3.shape
    nc = s // chunk
    wk, wv = heads * RET_DK, heads * RET_DV
    rq, rk = Z_RQ // wk, Z_RK // wk
    rv, gr = Z_RV // wv, Z_GR // wv
    return pl.pallas_call(
        functools.partial(_ret_kernel, heads=heads),
        grid=(b, RET_HEADS // heads, nc),
        in_specs=[
            pl.BlockSpec((None, chunk, wk), lambda b_, h, c: (b_, c, rq + h)),
            pl.BlockSpec((None, chunk, wk), lambda b_, h, c: (b_, c, rk + h)),
            pl.BlockSpec((None, chunk, wv), lambda b_, h, c: (b_, c, rv + h)),
            pl.BlockSpec((None, chunk, wv), lambda b_, h, c: (b_, c, gr + h)),
            pl.BlockSpec((BLOCK, wk), lambda b_, h, c: (0, rk + h)),
            pl.BlockSpec((BLOCK, wv), lambda b_, h, c: (0, rv + h)),
            pl.BlockSpec((1, wv), lambda b_, h, c: (0, h)),
        ],
        out_specs=pl.BlockSpec((None, chunk, wv), lambda b_, h, c: (b_, c, h)),
        out_shape=jax.ShapeDtypeStruct((b, s, RET_V_W), BF16),
        scratch_shapes=[pltpu.VMEM((heads, RET_DK, RET_DV), F32),
                        pltpu.VMEM((heads, chunk, chunk), F32),
                        pltpu.VMEM((heads, chunk, RET_DV), F32),
                        pltpu.VMEM((heads, chunk, RET_DK), F32),
                        pltpu.VMEM((heads, 1, RET_DV), F32)],
        compiler_params=_params(("parallel", "parallel", "arbitrary")),
        name="retention",
    )(z3, z3, z3, z3, zm, zm, ret_norm_g.reshape(1, RET_V_W))


def _merge_kernel(ya_ref, yr_ref, wf_ref, wr_ref, ga_ref, gr_ref, o_ref):
    a = jnp.dot(ya_ref[...], wf_ref[...], preferred_element_type=F32)
    r = jnp.dot(yr_ref[...], wr_ref[...], preferred_element_type=F32)
    o_ref[...] = (ga_ref[...].astype(F32) * a + gr_ref[...].astype(F32) * r).astype(o_ref.dtype)


def _merge(ya, yr, wf, wr, z, *, tm, tn, d_model):
    m = ya.shape[0]
    ga0 = Z_GA // tn
    gr0 = (Z_GA + d_model) // tn
    return pl.pallas_call(
        _merge_kernel,
        grid=(m // tm, d_model // tn),
        in_specs=[pl.BlockSpec((tm, FOX_W), lambda i, j: (i, 0)),
                  pl.BlockSpec((tm, RET_V_W), lambda i, j: (i, 0)),
                  pl.BlockSpec((FOX_W, tn), lambda i, j: (0, j)),
                  pl.BlockSpec((RET_V_W, tn), lambda i, j: (0, j)),
                  pl.BlockSpec((tm, tn), lambda i, j: (i, ga0 + j)),
                  pl.BlockSpec((tm, tn), lambda i, j: (i, gr0 + j))],
        out_specs=pl.BlockSpec((tm, tn), lambda i, j: (i, j)),
        out_shape=jax.ShapeDtypeStruct((m, d_model), BF16),
        compiler_params=_params(("parallel", "arbitrary"), V7X_VMEM_BUDGET),
        name="merge_proj",
    )(ya, yr, wf, wr, z, z)


def _outproj_kernel(a_ref, w_ref, x_ref, o_ref):
    o_ref[...] = x_ref[...] + jnp.dot(a_ref[...], w_ref[...], preferred_element_type=F32)


def _outproj(a, w, x, *, tm, tn):
    m, k = a.shape
    n = w.shape[1]
    return pl.pallas_call(
        _outproj_kernel,
        grid=(m // tm, n // tn),
        in_specs=[pl.BlockSpec((tm, k), lambda i, j: (i, 0)),
                  pl.BlockSpec((k, tn), lambda i, j: (0, j)),
                  pl.BlockSpec((tm, tn), lambda i, j: (i, j))],
        out_specs=pl.BlockSpec((tm, tn), lambda i, j: (i, j)),
        out_shape=jax.ShapeDtypeStruct((m, n), F32),
        compiler_params=_params(("parallel", "arbitrary"), V7X_VMEM_BUDGET),
        name="out_proj",
    )(a, w, x)


def _peer_score_kernel(x_ref, w_ref, keys_ref, o_ref):
    acc = jnp.dot(x_ref[...], w_ref[...], preferred_element_type=F32)
    for g in range(acc.shape[1] // PEER_HALF):
        qg = acc[:, g * PEER_HALF:(g + 1) * PEER_HALF].astype(BF16)
        o_ref[g] = _nt(keys_ref[g], qg)


def _peer_scores(xn, wq, keys, *, tm, tn):
    m, d = xn.shape
    n = wq.shape[1]
    ng = tn // PEER_HALF
    return pl.pallas_call(
        _peer_score_kernel,
        grid=(m // tm, n // tn),
        in_specs=[pl.BlockSpec((tm, d), lambda i, j: (i, 0)),
                  pl.BlockSpec((d, tn), lambda i, j: (0, j)),
                  pl.BlockSpec((ng, PEER_NKEYS, PEER_HALF), lambda i, j: (j, 0, 0))],
        out_specs=pl.BlockSpec((ng, PEER_NKEYS, tm), lambda i, j: (j, 0, i)),
        out_shape=jax.ShapeDtypeStruct((n // PEER_HALF, PEER_NKEYS, m), F32),
        compiler_params=_params(("parallel", "arbitrary"), V7X_VMEM_BUDGET),
        name="peer_scores",
    )(xn, wq, keys)


def _top_values(x, n, stacked_rows):
    rows, width = x.shape
    idx = lax.broadcasted_iota(jnp.int32, (rows, width), 0).astype(F32)
    srow = lax.broadcasted_iota(jnp.int32, (stacked_rows, width), 0)
    stacked = jnp.zeros((stacked_rows, width), F32)
    vals = []
    for t in range(n):
        cur = jnp.max(x, axis=0, keepdims=True)
        vals.append(cur)
        stacked = jnp.where(srow == t, cur, stacked)
        first = jnp.min(jnp.where(x == cur, idx, float(rows)), axis=0, keepdims=True)
        x = jnp.where(idx == first, -jnp.inf, x)
    return stacked, vals


def _peer_topk_kernel(s_ref, e1_ref, e2_ref, tau_ref):
    s1 = s_ref[0]
    s2 = s_ref[1]
    v1, v1l = _top_values(s1, PEER_TOPK, PEER_TOPK)
    v2, v2l = _top_values(s2, PEER_TOPK, PEER_TOPK)
    sub = lax.broadcasted_iota(jnp.int32, (8, s1.shape[1]), 0)
    pieces = [v1l[0] + v2]
    for a in range(1, 8):
        nb = PEER_TOPK // (a + 1)
        pieces.append(jnp.where(sub < nb, v1l[a] + v2[0:8], -jnp.inf))
    pieces.append(v1[8:16] + v2l[0])
    cand = jnp.concatenate(pieces, axis=0)
    _, top = _top_values(cand, PEER_TOPK, 8)
    m = top[0]
    z = jnp.zeros_like(m)
    for t in range(PEER_TOPK):
        z = z + jnp.exp(top[t] - m)
    tau_ref[...] = top[PEER_TOPK - 1]
    e1_ref[...] = jnp.exp(s1 - v1l[0])
    e2_ref[...] = jnp.exp(s2 - v2l[0]) / z


def _peer_topk(st, *, tt):
    _, _, t = st.shape
    return pl.pallas_call(
        _peer_topk_kernel,
        grid=(t // tt, PEER_HEADS),
        in_specs=[pl.BlockSpec((2, PEER_NKEYS, tt), lambda i, h: (h, 0, i))],
        out_specs=[pl.BlockSpec((None, PEER_NKEYS, tt), lambda i, h: (h, 0, i)),
                   pl.BlockSpec((None, PEER_NKEYS, tt), lambda i, h: (h, 0, i)),
                   pl.BlockSpec((None, 1, tt), lambda i, h: (h, 0, i))],
        out_shape=[jax.ShapeDtypeStruct((PEER_HEADS, PEER_NKEYS, t), F32),
                   jax.ShapeDtypeStruct((PEER_HEADS, PEER_NKEYS, t), F32),
                   jax.ShapeDtypeStruct((PEER_HEADS, 1, t), F32)],
        compiler_params=_params(("parallel", "parallel")),
        name="peer_topk",
    )(st)


def _gelu_exact(x):
    return 0.5 * x * (1.0 + lax.erf(x * (2.0 ** -0.5)))


def _peer_dense_kernel(xt_ref, u_ref, v_ref, s_ref, e1_ref, e2_ref, tau_ref, o_ref, gate_sc, *, na, halves):
    j = pl.program_id(1)

    def build_gate(chunk, gate_sc):
        sub = 8
        bshape = (sub, V7X_LANES)
        for a in range(na):
            row = chunk * na + a
            s1rows = [s_ref[2 * h, pl.ds(row, 1), :] for h in range(PEER_HEADS)]
            e1rows = [e1_ref[h, pl.ds(row, 1), :] for h in range(PEER_HEADS)]
            for l0 in range(0, gate_sc.shape[1], V7X_LANES):
                ls = slice(l0, l0 + V7X_LANES)
                s1b = [jnp.broadcast_to(s1rows[h][:, ls], bshape) for h in range(PEER_HEADS)]
                e1b = [jnp.broadcast_to(e1rows[h][:, ls], bshape) for h in range(PEER_HEADS)]
                taub = [jnp.broadcast_to(tau_ref[h, :, ls], bshape) for h in range(PEER_HEADS)]
                for r in range(0, PEER_NKEYS, sub):
                    gate = None
                    for h in range(PEER_HEADS):
                        sel = (s1b[h] + s_ref[2 * h + 1, r:r + sub, ls]) >= taub[h]
                        term = jnp.where(sel, e1b[h] * e2_ref[h, r:r + sub, ls], 0.0)
                        gate = term if gate is None else gate + term
                    gate_sc[a * PEER_NKEYS + r:a * PEER_NKEYS + r + sub, ls] = gate

    @pl.when(j == 0)
    def _():
        o_ref[...] = jnp.zeros_like(o_ref)

    build_gate(j, gate_sc)
    rows = na * PEER_NKEYS // halves
    total = None
    for k in range(halves):
        sl = slice(k * rows, (k + 1) * rows)
        act = jnp.dot(u_ref[sl, :], xt_ref[...], preferred_element_type=F32)
        w = (gate_sc[sl, :] * _gelu_exact(act)).astype(BF16)
        part = _tn(w, v_ref[sl, :])
        total = part if total is None else total + part
    o_ref[...] += total


def _peer_dense(xnt, u_tab, v_tab, st, e1, e2, tau, *, tt, ec):
    d, t = xnt.shape
    n_exp = u_tab.shape[0]
    na = ec // PEER_NKEYS
    once = pl.Buffered(1)
    kern = functools.partial(_peer_dense_kernel, na=na, halves=2)
    return pl.pallas_call(
        kern,
        grid=(t // tt, n_exp // ec),
        in_specs=[pl.BlockSpec((d, tt), lambda i, j: (0, i), pipeline_mode=once),
                  pl.BlockSpec((ec, d), lambda i, j: (j, 0)),
                  pl.BlockSpec((ec, d), lambda i, j: (j, 0)),
                  pl.BlockSpec((2 * PEER_HEADS, PEER_NKEYS, tt), lambda i, j: (0, 0, i), pipeline_mode=once),
                  pl.BlockSpec((PEER_HEADS, PEER_NKEYS, tt), lambda i, j: (0, 0, i), pipeline_mode=once),
                  pl.BlockSpec((PEER_HEADS, PEER_NKEYS, tt), lambda i, j: (0, 0, i), pipeline_mode=once),
                  pl.BlockSpec((PEER_HEADS, 1, tt), lambda i, j: (0, 0, i), pipeline_mode=once)],
        out_specs=pl.BlockSpec((tt, d), lambda i, j: (i, 0)),
        out_shape=jax.ShapeDtypeStruct((t, d), F32),
        scratch_shapes=[pltpu.VMEM((ec, tt), F32)],
        compiler_params=_params(("parallel", "arbitrary"), V7X_VMEM_BUDGET),
        name="peer_dense",
    )(xnt, u_tab, v_tab, st, e1, e2, tau)


def _add_kernel(a_ref, b_ref, o_ref):
    o_ref[...] = a_ref[...] + b_ref[...]


def _add(a, b, tm):
    m, d = a.shape
    spec = pl.BlockSpec((tm, d), lambda i: (i, 0))
    return pl.pallas_call(
        _add_kernel, grid=(m // tm,), in_specs=[spec, spec], out_specs=spec,
        out_shape=jax.ShapeDtypeStruct((m, d), a.dtype),
        compiler_params=_params(("parallel",), V7X_VMEM_BUDGET), name="residual_add",
    )(a, b)


def _rotary_tables(pos, valid):
    half = RET_DK // 2
    inv = ROPE_BASE ** (-jnp.arange(half, dtype=F32) / half)
    ang = pos[:, None] * inv[None, :]
    cos, sin = jnp.cos(ang), jnp.sin(ang)
    c = jnp.concatenate([cos, cos], axis=-1)
    s = jnp.concatenate([-sin, sin], axis=-1)
    kscale = (RET_DK ** -0.5) * valid.astype(F32)[:, None]
    return jnp.stack([c, c * kscale]), jnp.stack([s, s * kscale])


def _tile(n, pref):
    t = min(n, pref)
    assert n % t == 0
    return t


def kernel(x, meta_tokens, norm_mix_g, w_in, b_forget, q_norm_g, k_norm_g, ret_norm_g, w_proj_fox,
           w_proj_ret, w_out, norm_ffn_g, peer_w_q, peer_keys_1, peer_keys_2, peer_u, peer_v):
    b, s, d = x.shape
    depth = w_in.shape[0]
    assert depth == 1, "only the real-token outputs of a single layer are computed"
    m = b * s
    xr = x.reshape(m, d)
    hm = jnp.concatenate([jnp.zeros((PAD, d), x.dtype), meta_tokens.astype(x.dtype)], axis=0)

    w = w_in[0]
    o_fq, o_fk, o_fv, o_f = 0, FOX_W, 2 * FOX_W, 3 * FOX_W
    o_rq = o_f + FOX_HEADS
    o_rk, o_rv = o_rq + RET_QK_W, o_rq + 2 * RET_QK_W
    o_gr = o_rv + RET_V_W
    o_ga = o_gr + RET_V_W
    o_gb = o_ga + d
    col = lambda o, n: w[:, o:o + n]
    wp = jnp.concatenate([col(o_fq, FOX_W), col(o_fk, FOX_W), col(o_rq, RET_QK_W), col(o_rk, RET_QK_W),
                          col(o_fv, FOX_W), col(o_rv, RET_V_W), col(o_gr, RET_V_W), col(o_ga, d),
                          col(o_gb, d)], axis=1).astype(BF16)
    npk = wp.shape[1]
    wf = jnp.pad(col(o_f, FOX_HEADS), ((0, 0), (0, V7X_LANES - FOX_HEADS))).astype(BF16)
    bf = jnp.pad(b_forget[0], (0, V7X_LANES - FOX_HEADS)).reshape(1, V7X_LANES)
    gain = jnp.concatenate([jnp.tile(q_norm_g[0] * (FOX_HD ** -0.5), FOX_HEADS),
                            jnp.tile(k_norm_g[0], FOX_HEADS),
                            jnp.zeros((npk - 2 * FOX_W,), F32)]).reshape(1, npk)

    pos_real = jnp.arange(s, dtype=F32) + N_META
    rc, rs = _rotary_tables(pos_real, jnp.ones((s,), bool))
    pos_meta = jnp.arange(BLOCK, dtype=F32) - PAD
    rcm, rsm = _rotary_tables(pos_meta, jnp.arange(BLOCK) >= PAD)

    tm = _tile(m, 1024)
    tn = _tile(d, 1024)

    u = _rmsnorm_bf16(xr, norm_mix_g[0], _tile(m, 512))
    um = _rmsnorm_bf16(hm, norm_mix_g[0], BLOCK)
    z = _proj(u, wp, gain, rc, rs, tm=_tile(s, 1024), tn=tn, d_model=d, first_valid_row=0)
    zm = _proj(um, wp, gain, rcm, rsm, tm=BLOCK, tn=tn, d_model=d, first_valid_row=PAD)
    cm = _forget_cumsum(um, wf, bf, jnp.zeros((1, V7X_LANES), F32), nb=1, tm=BLOCK, first_valid_row=PAD)
    c = _forget_cumsum(u, wf, bf, cm[BLOCK - 1:BLOCK], nb=b, tm=_tile(s, 512), first_valid_row=0)
    c_rows = c.reshape(b, s, V7X_LANES)[:, :, :FOX_HEADS].transpose(0, 2, 1).reshape(b, FOX_HEADS, 1, s)
    cm_rows = cm[:, :FOX_HEADS].T.reshape(FOX_HEADS, 1, BLOCK)

    z3 = z.reshape(b, s, npk)
    ya = _fox_attention(z3, zm, c_rows, cm_rows, tq=_tile(s, 512), heads=4)
    chunk = _tile(s, 256)
    yr = _retention(z3, zm, ret_norm_g[0], chunk=chunk, heads=4)

    merged = _merge(ya.reshape(m, FOX_W), yr.reshape(m, RET_V_W), w_proj_fox[0].astype(BF16),
                    w_proj_ret[0].astype(BF16), z, tm=tm, tn=_tile(d, 512), d_model=d)
    h1 = _outproj(merged, w_out[0].astype(BF16), xr, tm=tm, tn=tn)

    xn = _rmsnorm_bf16(h1, norm_ffn_g[0], _tile(m, 512))
    keys = jnp.stack([peer_keys_1[0], peer_keys_2[0]], axis=1).reshape(
        2 * PEER_HEADS, PEER_NKEYS, PEER_HALF).astype(BF16)
    st = _peer_scores(xn, peer_w_q[0].astype(BF16), keys, tm=tm, tn=_tile(PEER_HEADS * PEER_QDIM, 1024))
    e1, e2, tau = _peer_topk(st, tt=_tile(m, 512))
    y = _peer_dense(xn.T, peer_u[0].astype(BF16), peer_v[0].astype(BF16), st, e1, e2, tau,
                    tt=_tile(m, 512), ec=512)
    return _add(h1, y, _tile(m, 512)).reshape(b, s, d)
```

```python
import functools

import jax
import jax.numpy as jnp
from jax import lax
from jax.experimental import pallas as pl
from jax.experimental.pallas import tpu as pltpu

N_META = 16
BLOCK = 128
PAD = BLOCK - N_META
EPS = 1e-6
NEG = -1e30
LOG2_E = 1.4426950408889634
FOX_HEADS = 16
FOX_HD = 128
FOX_W = FOX_HEADS * FOX_HD
RET_HEADS = 16
RET_DK = 128
RET_DV = 256
RET_QK_W = RET_HEADS * RET_DK
RET_V_W = RET_HEADS * RET_DV
ROPE_BASE = 10000.0
PEER_HEADS = 8
PEER_NKEYS = 128
PEER_N = PEER_NKEYS * PEER_NKEYS
PEER_QDIM = 256
PEER_HALF = PEER_QDIM // 2
PEER_TOPK = 16

V7X_LANES = 128
V7X_VMEM_BYTES = 64 * 1024 * 1024
V7X_VMEM_BUDGET = 56 * 1024 * 1024

BF16 = jnp.bfloat16
F32 = jnp.float32

Z_FQ = 0
Z_FK = Z_FQ + FOX_W
Z_RQ = Z_FK + FOX_W
Z_RK = Z_RQ + RET_QK_W
Z_FV = Z_RK + RET_QK_W
Z_RV = Z_FV + FOX_W
Z_GR = Z_RV + RET_V_W
Z_GA = Z_GR + RET_V_W


def _params(sem, vmem=None):
    return pltpu.CompilerParams(dimension_semantics=sem, vmem_limit_bytes=vmem)


def _nt(a, b):
    return lax.dot_general(a, b, (((1,), (1,)), ((), ())), preferred_element_type=F32)


def _tn(a, b):
    return lax.dot_general(a, b, (((0,), (0,)), ((), ())), preferred_element_type=F32)


def _sigmoid(x):
    return 1.0 / (1.0 + jnp.exp(-x))


def _rmsnorm_kernel(x_ref, g_ref, o_ref, *ot_ref):
    x = x_ref[...]
    ms = jnp.mean(x * x, axis=-1, keepdims=True)
    y = (x * lax.rsqrt(ms + EPS) * g_ref[...]).astype(o_ref.dtype)
    o_ref[...] = y
    if ot_ref:
        ot_ref[0][...] = y.T


def _rmsnorm_bf16(x, g, tm, with_transpose=False):
    m, d = x.shape
    out_specs = [pl.BlockSpec((tm, d), lambda i: (i, 0))]
    out_shape = [jax.ShapeDtypeStruct((m, d), BF16)]
    if with_transpose:
        out_specs.append(pl.BlockSpec((d, tm), lambda i: (0, i)))
        out_shape.append(jax.ShapeDtypeStruct((d, m), BF16))
    out = pl.pallas_call(
        _rmsnorm_kernel,
        grid=(m // tm,),
        in_specs=[pl.BlockSpec((tm, d), lambda i: (i, 0)),
                  pl.BlockSpec((1, d), lambda i: (0, 0))],
        out_specs=out_specs,
        out_shape=out_shape,
        compiler_params=_params(("parallel",), V7X_VMEM_BUDGET),
        name="rmsnorm_bf16",
    )(x, g.reshape(1, d))
    return out if with_transpose else out[0]


def _proj_kernel(u_ref, w_ref, gain_ref, rc_ref, rs_ref, o_ref, *, classes, first_valid_row):
    j = pl.program_id(1)
    acc = jnp.dot(u_ref[...], w_ref[...], preferred_element_type=F32)
    tm, tn = acc.shape
    groups = [slice(g * V7X_LANES, (g + 1) * V7X_LANES) for g in range(tn // V7X_LANES)]

    def emit(kind):
        if kind == "headnorm":
            for sl in groups:
                x = acc[:, sl]
                ms = jnp.mean(x * x, axis=-1, keepdims=True)
                o_ref[:, sl] = (x * lax.rsqrt(ms + EPS) * gain_ref[:, sl]).astype(o_ref.dtype)
        elif kind == "rotary":
            c = rc_ref[...]
            s = rs_ref[...]
            for sl in groups:
                x = acc[:, sl]
                o_ref[:, sl] = (x * c + pltpu.roll(x, V7X_LANES // 2, 1) * s).astype(o_ref.dtype)
        elif kind == "plain":
            o_ref[...] = acc.astype(o_ref.dtype)
        elif kind == "vmasked":
            if first_valid_row > 0:
                rows = lax.broadcasted_iota(jnp.int32, (tm, 1), 0) + pl.program_id(0) * tm
                o_ref[...] = jnp.where(rows >= first_valid_row, acc, 0.0).astype(o_ref.dtype)
            else:
                o_ref[...] = acc.astype(o_ref.dtype)
        elif kind == "silu":
            o_ref[...] = (acc * _sigmoid(acc)).astype(o_ref.dtype)
        elif kind == "sigmoid":
            o_ref[...] = _sigmoid(acc).astype(o_ref.dtype)
        else:
            raise ValueError(kind)

    for kind, lo, hi in classes:
        pl.when((j >= lo) & (j < hi))(functools.partial(emit, kind))


def _proj(u, wp, gain, rot_c, rot_s, *, tm, tn, d_model, first_valid_row):
    m, d = u.shape
    npk = wp.shape[1]
    n_row_tiles = rot_c.shape[1] // tm
    widths = (("headnorm", 2 * FOX_W), ("rotary", 2 * RET_QK_W), ("plain", FOX_W),
              ("vmasked", RET_V_W), ("silu", RET_V_W), ("sigmoid", 2 * d_model))
    classes, lo = [], 0
    for kind, w in widths:
        classes.append((kind, lo, lo + w // tn))
        lo += w // tn
    assert lo * tn == npk
    rk_lo = Z_RK // tn
    kern = functools.partial(_proj_kernel, classes=tuple(classes), first_valid_row=first_valid_row)
    rot_map = lambda i, j: ((j >= rk_lo).astype(jnp.int32), i % n_row_tiles, 0)
    return pl.pallas_call(
        kern,
        grid=(m // tm, npk // tn),
        in_specs=[pl.BlockSpec((tm, d), lambda i, j: (i, 0)),
                  pl.BlockSpec((d, tn), lambda i, j: (0, j)),
                  pl.BlockSpec((1, tn), lambda i, j: (0, j)),
                  pl.BlockSpec((None, tm, V7X_LANES), rot_map),
                  pl.BlockSpec((None, tm, V7X_LANES), rot_map)],
        out_specs=pl.BlockSpec((tm, tn), lambda i, j: (i, j)),
        out_shape=jax.ShapeDtypeStruct((m, npk), BF16),
        compiler_params=_params(("parallel", "arbitrary"), V7X_VMEM_BUDGET),
        name="input_proj",
    )(u, wp, gain, rot_c, rot_s)


def _split3(x):
    hi = x.astype(BF16)
    r = x - hi.astype(F32)
    mid = r.astype(BF16)
    lo = (r - mid.astype(F32)).astype(BF16)
    return hi, mid, lo


def _forget_kernel(u_ref, w_ref, b_ref, cin_ref, selq_ref, selk_ref, o_ref, cqx_ref, ckx_ref, carry_ref, *,
                   first_valid_row):
    i = pl.program_id(1)

    @pl.when(i == 0)
    def _():
        carry_ref[...] = cin_ref[...]

    x = jnp.dot(u_ref[...], w_ref[...], preferred_element_type=F32) + b_ref[...]
    tm = x.shape[0]
    logf = jnp.minimum(x, 0.0) - jnp.log1p(jnp.exp(-jnp.abs(x)))
    rows = lax.broadcasted_iota(jnp.int32, (tm, tm), 0)
    cols = lax.broadcasted_iota(jnp.int32, (tm, tm), 1)
    if first_valid_row > 0:
        r1 = lax.broadcasted_iota(jnp.int32, (tm, 1), 0) + i * tm
        logf = jnp.where(r1 >= first_valid_row, logf, 0.0)
    tri = jnp.where(rows >= cols, 1.0, 0.0).astype(BF16)
    hi, mid, lo = _split3(logf)
    cs = (jnp.dot(tri, hi, preferred_element_type=F32)
          + jnp.dot(tri, mid, preferred_element_type=F32)
          + jnp.dot(tri, lo, preferred_element_type=F32))
    c = cs + carry_ref[...]
    o_ref[...] = c
    carry_ref[...] = c[tm - 1:tm, :]
    chi, cmid, clo = _split3(c * LOG2_E)
    lane = lax.broadcasted_iota(jnp.int32, c.shape, 1)
    one = jnp.where(lane < FOX_HEADS, 1.0, 0.0).astype(BF16)
    pq = jnp.concatenate([chi, cmid, clo, one, one, one], axis=1)
    pk = jnp.concatenate([one, one, one, -chi, -cmid, -clo], axis=1)
    cqx_ref[...] = jnp.dot(pq, selq_ref[...], preferred_element_type=F32).astype(BF16)
    ckx_ref[...] = jnp.dot(pk, selk_ref[...], preferred_element_type=F32).astype(BF16)


FOX_EXT = 6


def _forget_selectors():
    import numpy as np
    selq = np.zeros((FOX_EXT * V7X_LANES, V7X_LANES), np.float32)
    selk = np.zeros((FOX_EXT * V7X_LANES, FOX_W), np.float32)
    for p in range(FOX_EXT):
        for h in range(FOX_HEADS):
            selq[p * V7X_LANES + h, FOX_EXT * h + p] = 1.0
            selk[p * V7X_LANES + h, FOX_HD * h + FOX_EXT * h + p] = 1.0
    return jnp.asarray(selq, BF16), jnp.asarray(selk, BF16)


def _forget_cumsum(u, wf, bf, carry_in, *, nb, tm, first_valid_row):
    m, d = u.shape
    nt = m // (nb * tm)
    selq, selk = _forget_selectors()
    kern = functools.partial(_forget_kernel, first_valid_row=first_valid_row)
    const = lambda b, i: (0, 0)
    row = lambda b, i: (b * nt + i, 0)
    return pl.pallas_call(
        kern,
        grid=(nb, nt),
        in_specs=[pl.BlockSpec((tm, d), row),
                  pl.BlockSpec((d, V7X_LANES), const),
                  pl.BlockSpec((1, V7X_LANES), const),
                  pl.BlockSpec((1, V7X_LANES), const),
                  pl.BlockSpec(selq.shape, const),
                  pl.BlockSpec(selk.shape, const)],
        out_specs=[pl.BlockSpec((tm, V7X_LANES), row),
                   pl.BlockSpec((tm, V7X_LANES), row),
                   pl.BlockSpec((tm, FOX_W), row)],
        out_shape=[jax.ShapeDtypeStruct((m, V7X_LANES), F32),
                   jax.ShapeDtypeStruct((m, V7X_LANES), BF16),
                   jax.ShapeDtypeStruct((m, FOX_W), BF16)],
        scratch_shapes=[pltpu.VMEM((1, V7X_LANES), F32)],
        compiler_params=_params(("parallel", "arbitrary"), V7X_VMEM_BUDGET),
        name="forget_cumsum",
    )(u, wf, bf, carry_in, selq, selk)


def _fox_kernel(qi_ref, kj_ref, kind_ref, q_ref, cqx_ref, km_ref, vm_ref, ckm_ref, k_ref, v_ref,
                ck_ref, o_ref, m_sc, acc_sc, qx_sc, *, heads):
    t = pl.program_id(2)
    kind = kind_ref[t]
    hd = FOX_HD
    lanes = [slice(g * hd, (g + 1) * hd) for g in range(heads)]

    def scores(g, k_r, ck_r):
        sl = lanes[g]
        k_ext = jnp.concatenate([k_r[:, sl], ck_r[:, sl]], axis=1)
        return _nt(k_ext, qx_sc[g])

    def pv(p, v):
        ones = jnp.ones((16, v.shape[0]), v.dtype)
        return jnp.dot(jnp.concatenate([v.T, ones], axis=0), p, preferred_element_type=F32)

    def update(g, s, v):
        m_prev = m_sc[g]
        m_new = jnp.maximum(m_prev, jnp.max(s, axis=0, keepdims=True))
        alpha = jnp.exp2(m_prev - m_new)
        p = jnp.exp2(s - m_new).astype(BF16)
        acc_sc[g] = alpha * acc_sc[g] + pv(p, v)
        m_sc[g] = m_new

    @pl.when(kind == 0)
    def _():
        for g, sl in enumerate(lanes):
            qx_sc[g] = jnp.concatenate([q_ref[:, sl], cqx_ref[...]], axis=1)
            s = scores(g, km_ref, ckm_ref)
            kpos = lax.broadcasted_iota(jnp.int32, s.shape, 0)
            s = jnp.where(kpos >= PAD, s, NEG)
            m = jnp.max(s, axis=0, keepdims=True)
            p = jnp.exp2(s - m).astype(BF16)
            m_sc[g] = m
            acc_sc[g] = pv(p, vm_ref[:, sl])

    @pl.when(kind == 1)
    def _():
        for g, sl in enumerate(lanes):
            update(g, scores(g, k_ref, ck_ref), v_ref[:, sl])

    @pl.when(kind == 2)
    def _():
        for g, sl in enumerate(lanes):
            s = scores(g, k_ref, ck_ref)
            krow = lax.broadcasted_iota(jnp.int32, s.shape, 0)
            qcol = lax.broadcasted_iota(jnp.int32, s.shape, 1)
            s = jnp.where(krow <= qcol, s, NEG)
            update(g, s, v_ref[:, sl])
            acc = acc_sc[g]
            o_ref[:, sl] = (acc[:hd, :] / acc[hd:hd + 1, :]).T.astype(o_ref.dtype)


def _fox_attention(z3, zm, cqx, ckx, ckxm, *, tq, heads):
    b, s, _ = z3.shape
    nq = s // tq
    qi, kj, kind = [], [], []
    for i in range(nq):
        qi += [i] * (i + 2)
        kj += [0] + list(range(i + 1))
        kind += [0] + [1] * i + [2]
    n_steps = len(qi)
    qi = jnp.asarray(qi, jnp.int32)
    kj = jnp.asarray(kj, jnp.int32)
    kind = jnp.asarray(kind, jnp.int32)
    w = heads * FOX_HD
    fq, fk, fv = Z_FQ // w, Z_FK // w, Z_FV // w
    grid_spec = pltpu.PrefetchScalarGridSpec(
        num_scalar_prefetch=3,
        grid=(b, FOX_HEADS // heads, n_steps),
        in_specs=[
            pl.BlockSpec((None, tq, w), lambda b_, h, t, qi, kj, kd: (b_, qi[t], fq + h)),
            pl.BlockSpec((None, tq, V7X_LANES), lambda b_, h, t, qi, kj, kd: (b_, qi[t], 0)),
            pl.BlockSpec((BLOCK, w), lambda b_, h, t, qi, kj, kd: (0, fk + h)),
            pl.BlockSpec((BLOCK, w), lambda b_, h, t, qi, kj, kd: (0, fv + h)),
            pl.BlockSpec((BLOCK, w), lambda b_, h, t, qi, kj, kd: (0, h)),
            pl.BlockSpec((None, tq, w), lambda b_, h, t, qi, kj, kd: (b_, kj[t], fk + h)),
            pl.BlockSpec((None, tq, w), lambda b_, h, t, qi, kj, kd: (b_, kj[t], fv + h)),
            pl.BlockSpec((None, tq, w), lambda b_, h, t, qi, kj, kd: (b_, kj[t], h)),
        ],
        out_specs=pl.BlockSpec((None, tq, w), lambda b_, h, t, qi, kj, kd: (b_, qi[t], h)),
        scratch_shapes=[pltpu.VMEM((heads, 1, tq), F32),
                        pltpu.VMEM((heads, FOX_HD + 16, tq), F32),
                        pltpu.VMEM((heads, tq, 2 * FOX_HD), BF16)],
    )
    return pl.pallas_call(
        functools.partial(_fox_kernel, heads=heads),
        grid_spec=grid_spec,
        out_shape=jax.ShapeDtypeStruct((b, s, FOX_W), BF16),
        compiler_params=_params(("parallel", "parallel", "arbitrary")),
        name="fox_attention",
    )(qi, kj, kind, z3, cqx, zm, zm, ckxm, z3, z3, ckx)


def _ret_kernel(q_ref, k_ref, v_ref, g_ref, km_ref, vm_ref, gn_ref, o_ref, state_sc, decay_sc,
                xi_sc, zeta_sc, cd_sc, *, heads):
    hg = pl.program_id(1)
    c = pl.program_id(2)
    chunk = q_ref.shape[0]
    dk, dv = RET_DK, RET_DV

    def log_gamma(head, width):
        hv = jnp.full((1, width), head, jnp.int32).astype(F32)
        return jnp.log1p(-jnp.exp2(-5.0 - hv))

    def rows(n, width):
        return lax.broadcasted_iota(jnp.int32, (n, width), 0).astype(F32)

    @pl.when(c == 0)
    def _():
        for g in range(heads):
            head = hg * heads + g
            diff = rows(chunk, chunk) - lax.broadcasted_iota(jnp.int32, (chunk, chunk), 1).astype(F32)
            decay_sc[g] = jnp.where(diff >= 0, jnp.exp(log_gamma(head, chunk) * jnp.maximum(diff, 0.0)), 0.0)
            xi_sc[g] = jnp.exp(log_gamma(head, dv) * (rows(chunk, dv) + 1.0))
            zeta_sc[g] = jnp.exp(log_gamma(head, dk) * (chunk - 1.0 - rows(chunk, dk)))
            cd_sc[g] = jnp.exp(log_gamma(head, dv) * float(chunk))
            zeta_m = jnp.exp(log_gamma(head, dk) * (BLOCK - 1.0 - rows(BLOCK, dk)))
            kz = (km_ref[:, g * dk:(g + 1) * dk].astype(F32) * zeta_m).astype(BF16)
            state_sc[g] = _tn(kz, vm_ref[:, g * dv:(g + 1) * dv])

    for g in range(heads):
        q = q_ref[:, g * dk:(g + 1) * dk]
        k = k_ref[:, g * dk:(g + 1) * dk]
        v = v_ref[:, g * dv:(g + 1) * dv]
        s = _nt(q, k) * decay_sc[g]
        intra = jnp.dot(s.astype(BF16), v, preferred_element_type=F32)
        state = state_sc[g]
        cross = jnp.dot(q, state.astype(BF16), preferred_element_type=F32) * xi_sc[g]
        o = intra + cross
        kz = (k.astype(F32) * zeta_sc[g]).astype(BF16)
        state_sc[g] = state * cd_sc[g] + _tn(kz, v)
        mu = jnp.mean(o, axis=-1, keepdims=True)
        d = o - mu
        var = jnp.mean(d * d, axis=-1, keepdims=True)
        y = d * lax.rsqrt(var + EPS) * gn_ref[:, g * dv:(g + 1) * dv]
        o_ref[:, g * dv:(g + 1) * dv] = (g_ref[:, g * dv:(g + 1) * dv].astype(F32) * y).astype(o_ref.dtype)


def _retention(z3, zm, ret_norm_g, *, chunk, heads):
    b, s, _ = z3.shape
    nc = s // chunk
    wk, wv = heads * RET_DK, heads * RET_DV
    rq, rk = Z_RQ // wk, Z_RK // wk
    rv, gr = Z_RV // wv, Z_GR // wv
    return pl.pallas_call(
        functools.partial(_ret_kernel, heads=heads),
        grid=(b, RET_HEADS // heads, nc),
        in_specs=[
            pl.BlockSpec((None, chunk, wk), lambda b_, h, c: (b_, c, rq + h)),
            pl.BlockSpec((None, chunk, wk), lambda b_, h, c: (b_, c, rk + h)),
            pl.BlockSpec((None, chunk, wv), lambda b_, h, c: (b_, c, rv + h)),
            pl.BlockSpec((None, chunk, wv), lambda b_, h, c: (b_, c, gr + h)),
            pl.BlockSpec((BLOCK, wk), lambda b_, h, c: (0, rk + h)),
            pl.BlockSpec((BLOCK, wv), lambda b_, h, c: (0, rv + h)),
            pl.BlockSpec((1, wv), lambda b_, h, c: (0, h)),
        ],
        out_specs=pl.BlockSpec((None, chunk, wv), lambda b_, h, c: (b_, c, h)),
        out_shape=jax.ShapeDtypeStruct((b, s, RET_V_W), BF16),
        scratch_shapes=[pltpu.VMEM((heads, RET_DK, RET_DV), F32),
                        pltpu.VMEM((heads, chunk, chunk), F32),
                        pltpu.VMEM((heads, chunk, RET_DV), F32),
                        pltpu.VMEM((heads, chunk, RET_DK), F32),
                        pltpu.VMEM((heads, 1, RET_DV), F32)],
        compiler_params=_params(("parallel", "parallel", "arbitrary")),
        name="retention",
    )(z3, z3, z3, z3, zm, zm, ret_norm_g.reshape(1, RET_V_W))


def _merge_kernel(ya_ref, yr_ref, wf_ref, wr_ref, ga_ref, gr_ref, o_ref):
    a = jnp.dot(ya_ref[...], wf_ref[...], preferred_element_type=F32)
    r = jnp.dot(yr_ref[...], wr_ref[...], preferred_element_type=F32)
    o_ref[...] = (ga_ref[...].astype(F32) * a + gr_ref[...].astype(F32) * r).astype(o_ref.dtype)


def _merge(ya, yr, wf, wr, z, *, tm, tn, d_model):
    m = ya.shape[0]
    ga0 = Z_GA // tn
    gr0 = (Z_GA + d_model) // tn
    return pl.pallas_call(
        _merge_kernel,
        grid=(m // tm, d_model // tn),
        in_specs=[pl.BlockSpec((tm, FOX_W), lambda i, j: (i, 0)),
                  pl.BlockSpec((tm, RET_V_W), lambda i, j: (i, 0)),
                  pl.BlockSpec((FOX_W, tn), lambda i, j: (0, j)),
                  pl.BlockSpec((RET_V_W, tn), lambda i, j: (0, j)),
                  pl.BlockSpec((tm, tn), lambda i, j: (i, ga0 + j)),
                  pl.BlockSpec((tm, tn), lambda i, j: (i, gr0 + j))],
        out_specs=pl.BlockSpec((tm, tn), lambda i, j: (i, j)),
        out_shape=jax.ShapeDtypeStruct((m, d_model), BF16),
        compiler_params=_params(("parallel", "arbitrary"), V7X_VMEM_BUDGET),
        name="merge_proj",
    )(ya, yr, wf, wr, z, z)


def _outproj_kernel(a_ref, w_ref, x_ref, o_ref):
    o_ref[...] = x_ref[...] + jnp.dot(a_ref[...], w_ref[...], preferred_element_type=F32)


def _outproj(a, w, x, *, tm, tn):
    m, k = a.shape
    n = w.shape[1]
    return pl.pallas_call(
        _outproj_kernel,
        grid=(m // tm, n // tn),
        in_specs=[pl.BlockSpec((tm, k), lambda i, j: (i, 0)),
                  pl.BlockSpec((k, tn), lambda i, j: (0, j)),
                  pl.BlockSpec((tm, tn), lambda i, j: (i, j))],
        out_specs=pl.BlockSpec((tm, tn), lambda i, j: (i, j)),
        out_shape=jax.ShapeDtypeStruct((m, n), F32),
        compiler_params=_params(("parallel", "arbitrary"), V7X_VMEM_BUDGET),
        name="out_proj",
    )(a, w, x)


def _peer_score_kernel(x_ref, w_ref, keys_ref, o_ref):
    acc = jnp.dot(x_ref[...], w_ref[...], preferred_element_type=F32)
    for g in range(acc.shape[1] // PEER_HALF):
        qg = acc[:, g * PEER_HALF:(g + 1) * PEER_HALF].astype(BF16)
        o_ref[g] = _nt(keys_ref[g], qg)


def _peer_scores(xn, wq, keys, *, tm, tn):
    m, d = xn.shape
    n = wq.shape[1]
    ng = tn // PEER_HALF
    return pl.pallas_call(
        _peer_score_kernel,
        grid=(m // tm, n // tn),
        in_specs=[pl.BlockSpec((tm, d), lambda i, j: (i, 0)),
                  pl.BlockSpec((d, tn), lambda i, j: (0, j)),
                  pl.BlockSpec((ng, PEER_NKEYS, PEER_HALF), lambda i, j: (j, 0, 0))],
        out_specs=pl.BlockSpec((ng, PEER_NKEYS, tm), lambda i, j: (j, 0, i)),
        out_shape=jax.ShapeDtypeStruct((n // PEER_HALF, PEER_NKEYS, m), F32),
        compiler_params=_params(("parallel", "arbitrary"), V7X_VMEM_BUDGET),
        name="peer_scores",
    )(xn, wq, keys)


def _top_values(x, n, stacked_rows):
    rows, width = x.shape
    idx = lax.broadcasted_iota(jnp.int32, (rows, width), 0).astype(F32)
    srow = lax.broadcasted_iota(jnp.int32, (stacked_rows, width), 0)
    stacked = jnp.zeros((stacked_rows, width), F32)
    vals = []
    for t in range(n):
        cur = jnp.max(x, axis=0, keepdims=True)
        vals.append(cur)
        stacked = jnp.where(srow == t, cur, stacked)
        first = jnp.min(jnp.where(x == cur, idx, float(rows)), axis=0, keepdims=True)
        x = jnp.where(idx == first, -jnp.inf, x)
    return stacked, vals


def _peer_topk_kernel(s_ref, e1_ref, e2_ref, tau_ref):
    s1 = s_ref[0]
    s2 = s_ref[1]
    v1, v1l = _top_values(s1, PEER_TOPK, PEER_TOPK)
    v2, v2l = _top_values(s2, PEER_TOPK, PEER_TOPK)
    sub = lax.broadcasted_iota(jnp.int32, (8, s1.shape[1]), 0)
    pieces = [v1l[0] + v2]
    for a in range(1, 8):
        nb = PEER_TOPK // (a + 1)
        pieces.append(jnp.where(sub < nb, v1l[a] + v2[0:8], -jnp.inf))
    pieces.append(v1[8:16] + v2l[0])
    cand = jnp.concatenate(pieces, axis=0)
    _, top = _top_values(cand, PEER_TOPK, 8)
    m = top[0]
    z = jnp.zeros_like(m)
    for t in range(PEER_TOPK):
        z = z + jnp.exp(top[t] - m)
    tau_ref[...] = top[PEER_TOPK - 1]
    e1_ref[...] = jnp.exp(s1 - v1l[0])
    e2_ref[...] = jnp.exp(s2 - v2l[0]) / z


def _peer_topk(st, *, tt):
    _, _, t = st.shape
    return pl.pallas_call(
        _peer_topk_kernel,
        grid=(t // tt, PEER_HEADS),
        in_specs=[pl.BlockSpec((2, PEER_NKEYS, tt), lambda i, h: (h, 0, i))],
        out_specs=[pl.BlockSpec((None, PEER_NKEYS, tt), lambda i, h: (h, 0, i)),
                   pl.BlockSpec((None, PEER_NKEYS, tt), lambda i, h: (h, 0, i)),
                   pl.BlockSpec((None, 1, tt), lambda i, h: (h, 0, i))],
        out_shape=[jax.ShapeDtypeStruct((PEER_HEADS, PEER_NKEYS, t), F32),
                   jax.ShapeDtypeStruct((PEER_HEADS, PEER_NKEYS, t), F32),
                   jax.ShapeDtypeStruct((PEER_HEADS, 1, t), F32)],
        compiler_params=_params(("parallel", "parallel")),
        name="peer_topk",
    )(st)


def _gelu_exact(x):
    return 0.5 * x * (1.0 + lax.erf(x * (2.0 ** -0.5)))


def _peer_dense_kernel(xt_ref, u_ref, v_ref, s_ref, e1_ref, e2_ref, tau_ref, h_ref, o_ref, gate_sc, *,
                       na, halves):
    j = pl.program_id(1)

    def build_gate(chunk, gate_sc):
        sub = 8
        bshape = (sub, V7X_LANES)
        for a in range(na):
            row = chunk * na + a
            s1rows = [s_ref[2 * h, pl.ds(row, 1), :] for h in range(PEER_HEADS)]
            e1rows = [e1_ref[h, pl.ds(row, 1), :] for h in range(PEER_HEADS)]
            for l0 in range(0, gate_sc.shape[1], V7X_LANES):
                ls = slice(l0, l0 + V7X_LANES)
                s1b = [jnp.broadcast_to(s1rows[h][:, ls], bshape) for h in range(PEER_HEADS)]
                e1b = [jnp.broadcast_to(e1rows[h][:, ls], bshape) for h in range(PEER_HEADS)]
                taub = [jnp.broadcast_to(tau_ref[h, :, ls], bshape) for h in range(PEER_HEADS)]
                for r in range(0, PEER_NKEYS, sub):
                    gate = None
                    for h in range(PEER_HEADS):
                        sel = (s1b[h] + s_ref[2 * h + 1, r:r + sub, ls]) >= taub[h]
                        term = jnp.where(sel, e1b[h] * e2_ref[h, r:r + sub, ls], 0.0)
                        gate = term if gate is None else gate + term
                    gate_sc[a * PEER_NKEYS + r:a * PEER_NKEYS + r + sub, ls] = gate

    @pl.when(j == 0)
    def _():
        o_ref[...] = h_ref[...]

    build_gate(j, gate_sc)
    rows = na * PEER_NKEYS // halves
    total = None
    for k in range(halves):
        sl = slice(k * rows, (k + 1) * rows)
        act = jnp.dot(u_ref[sl, :], xt_ref[...], preferred_element_type=F32)
        w = (gate_sc[sl, :] * _gelu_exact(act)).astype(BF16)
        part = _tn(w, v_ref[sl, :])
        total = part if total is None else total + part
    o_ref[...] += total


def _peer_dense(xnt, u_tab, v_tab, st, e1, e2, tau, h, *, tt, ec):
    d, t = xnt.shape
    n_exp = u_tab.shape[0]
    na = ec // PEER_NKEYS
    once = pl.Buffered(1)
    kern = functools.partial(_peer_dense_kernel, na=na, halves=2)
    return pl.pallas_call(
        kern,
        grid=(t // tt, n_exp // ec),
        in_specs=[pl.BlockSpec((d, tt), lambda i, j: (0, i), pipeline_mode=once),
                  pl.BlockSpec((ec, d), lambda i, j: (j, 0)),
                  pl.BlockSpec((ec, d), lambda i, j: (j, 0)),
                  pl.BlockSpec((2 * PEER_HEADS, PEER_NKEYS, tt), lambda i, j: (0, 0, i), pipeline_mode=once),
                  pl.BlockSpec((PEER_HEADS, PEER_NKEYS, tt), lambda i, j: (0, 0, i), pipeline_mode=once),
                  pl.BlockSpec((PEER_HEADS, PEER_NKEYS, tt), lambda i, j: (0, 0, i), pipeline_mode=once),
                  pl.BlockSpec((PEER_HEADS, 1, tt), lambda i, j: (0, 0, i), pipeline_mode=once),
                  pl.BlockSpec((tt, d), lambda i, j: (i, 0), pipeline_mode=once)],
        out_specs=pl.BlockSpec((tt, d), lambda i, j: (i, 0)),
        out_shape=jax.ShapeDtypeStruct((t, d), F32),
        scratch_shapes=[pltpu.VMEM((ec, tt), F32)],
        compiler_params=_params(("parallel", "arbitrary"), V7X_VMEM_BYTES - 4 * 1024 * 1024),
        name="peer_dense",
    )(xnt, u_tab, v_tab, st, e1, e2, tau, h)


def _rotary_tables(pos, valid):
    half = RET_DK // 2
    inv = ROPE_BASE ** (-jnp.arange(half, dtype=F32) / half)
    ang = pos[:, None] * inv[None, :]
    cos, sin = jnp.cos(ang), jnp.sin(ang)
    c = jnp.concatenate([cos, cos], axis=-1)
    s = jnp.concatenate([-sin, sin], axis=-1)
    kscale = (RET_DK ** -0.5) * valid.astype(F32)[:, None]
    return jnp.stack([c, c * kscale]), jnp.stack([s, s * kscale])


def _tile(n, pref):
    t = min(n, pref)
    assert n % t == 0
    return t


def kernel(x, meta_tokens, norm_mix_g, w_in, b_forget, q_norm_g, k_norm_g, ret_norm_g, w_proj_fox,
           w_proj_ret, w_out, norm_ffn_g, peer_w_q, peer_keys_1, peer_keys_2, peer_u, peer_v):
    b, s, d = x.shape
    depth = w_in.shape[0]
    assert depth == 1, "only the real-token outputs of a single layer are computed"
    m = b * s
    xr = x.reshape(m, d)
    hm = jnp.concatenate([jnp.zeros((PAD, d), x.dtype), meta_tokens.astype(x.dtype)], axis=0)

    w = w_in[0]
    o_fq, o_fk, o_fv, o_f = 0, FOX_W, 2 * FOX_W, 3 * FOX_W
    o_rq = o_f + FOX_HEADS
    o_rk, o_rv = o_rq + RET_QK_W, o_rq + 2 * RET_QK_W
    o_gr = o_rv + RET_V_W
    o_ga = o_gr + RET_V_W
    o_gb = o_ga + d
    col = lambda o, n: w[:, o:o + n]
    wp = jnp.concatenate([col(o_fq, FOX_W), col(o_fk, FOX_W), col(o_rq, RET_QK_W), col(o_rk, RET_QK_W),
                          col(o_fv, FOX_W), col(o_rv, RET_V_W), col(o_gr, RET_V_W), col(o_ga, d),
                          col(o_gb, d)], axis=1).astype(BF16)
    npk = wp.shape[1]
    wf = jnp.pad(col(o_f, FOX_HEADS), ((0, 0), (0, V7X_LANES - FOX_HEADS))).astype(BF16)
    bf = jnp.pad(b_forget[0], (0, V7X_LANES - FOX_HEADS)).reshape(1, V7X_LANES)
    gain = jnp.concatenate([jnp.tile(q_norm_g[0] * (FOX_HD ** -0.5 * LOG2_E), FOX_HEADS),
                            jnp.tile(k_norm_g[0], FOX_HEADS),
                            jnp.zeros((npk - 2 * FOX_W,), F32)]).reshape(1, npk)

    pos_real = jnp.arange(s, dtype=F32) + N_META
    rc, rs = _rotary_tables(pos_real, jnp.ones((s,), bool))
    pos_meta = jnp.arange(BLOCK, dtype=F32) - PAD
    rcm, rsm = _rotary_tables(pos_meta, jnp.arange(BLOCK) >= PAD)

    tm = _tile(m, 1024)
    tn = _tile(d, 1024)

    u = _rmsnorm_bf16(xr, norm_mix_g[0], _tile(m, 512))
    um = _rmsnorm_bf16(hm, norm_mix_g[0], BLOCK)
    z = _proj(u, wp, gain, rc, rs, tm=_tile(s, 1024), tn=tn, d_model=d, first_valid_row=0)
    zm = _proj(um, wp, gain, rcm, rsm, tm=BLOCK, tn=tn, d_model=d, first_valid_row=PAD)
    cm, _, ckxm = _forget_cumsum(um, wf, bf, jnp.zeros((1, V7X_LANES), F32), nb=1, tm=BLOCK,
                                 first_valid_row=PAD)
    _, cqx, ckx = _forget_cumsum(u, wf, bf, cm[BLOCK - 1:BLOCK], nb=b, tm=_tile(s, 512), first_valid_row=0)

    z3 = z.reshape(b, s, npk)
    ya = _fox_attention(z3, zm, cqx.reshape(b, s, V7X_LANES), ckx.reshape(b, s, FOX_W), ckxm,
                        tq=_tile(s, 512), heads=8)
    chunk = _tile(s, 256)
    yr = _retention(z3, zm, ret_norm_g[0], chunk=chunk, heads=4)

    merged = _merge(ya.reshape(m, FOX_W), yr.reshape(m, RET_V_W), w_proj_fox[0].astype(BF16),
                    w_proj_ret[0].astype(BF16), z, tm=tm, tn=_tile(d, 512), d_model=d)
    h1 = _outproj(merged, w_out[0].astype(BF16), xr, tm=tm, tn=tn)

    xn, xnt = _rmsnorm_bf16(h1, norm_ffn_g[0], _tile(m, 512), with_transpose=True)
    keys = jnp.stack([peer_keys_1[0], peer_keys_2[0]], axis=1).reshape(
        2 * PEER_HEADS, PEER_NKEYS, PEER_HALF).astype(BF16)
    st = _peer_scores(xn, peer_w_q[0].astype(BF16), keys, tm=tm, tn=_tile(PEER_HEADS * PEER_QDIM, 1024))
    e1, e2, tau = _peer_topk(st, tt=_tile(m, 512))
    out = _peer_dense(xnt, peer_u[0].astype(BF16), peer_v[0].astype(BF16), st, e1, e2, tau, h1,
                      tt=_tile(m, 512), ec=512)
    return out.reshape(b, s, d)
```

```python
import functools

import jax
import jax.numpy as jnp
from jax import lax
from jax.experimental import pallas as pl
from jax.experimental.pallas import tpu as pltpu

N_META = 16
BLOCK = 128
PAD = BLOCK - N_META
EPS = 1e-6
NEG = -1e30
LOG2_E = 1.4426950408889634
FOX_HEADS = 16
FOX_HD = 128
FOX_W = FOX_HEADS * FOX_HD
RET_HEADS = 16
RET_DK = 128
RET_DV = 256
RET_QK_W = RET_HEADS * RET_DK
RET_V_W = RET_HEADS * RET_DV
ROPE_BASE = 10000.0
PEER_HEADS = 8
PEER_NKEYS = 128
PEER_N = PEER_NKEYS * PEER_NKEYS
PEER_QDIM = 256
PEER_HALF = PEER_QDIM // 2
PEER_TOPK = 16

V7X_LANES = 128
V7X_VMEM_BYTES = 64 * 1024 * 1024
V7X_VMEM_BUDGET = 56 * 1024 * 1024

BF16 = jnp.bfloat16
F32 = jnp.float32

Z_FQ = 0
Z_FK = Z_FQ + FOX_W
Z_RQ = Z_FK + FOX_W
Z_RK = Z_RQ + RET_QK_W
Z_FV = Z_RK + RET_QK_W
Z_RV = Z_FV + FOX_W
Z_GR = Z_RV + RET_V_W
Z_GA = Z_GR + RET_V_W


def _params(sem, vmem=None):
    return pltpu.CompilerParams(dimension_semantics=sem, vmem_limit_bytes=vmem)


def _nt(a, b):
    return lax.dot_general(a, b, (((1,), (1,)), ((), ())), preferred_element_type=F32)


def _tn(a, b):
    return lax.dot_general(a, b, (((0,), (0,)), ((), ())), preferred_element_type=F32)


def _sigmoid(x):
    return 1.0 / (1.0 + jnp.exp(-x))


def _rmsnorm_kernel(x_ref, g_ref, o_ref, *ot_ref):
    x = x_ref[...]
    ms = jnp.mean(x * x, axis=-1, keepdims=True)
    y = (x * lax.rsqrt(ms + EPS) * g_ref[...]).astype(o_ref.dtype)
    o_ref[...] = y
    if ot_ref:
        ot_ref[0][...] = y.T


def _rmsnorm_bf16(x, g, tm, with_transpose=False):
    m, d = x.shape
    out_specs = [pl.BlockSpec((tm, d), lambda i: (i, 0))]
    out_shape = [jax.ShapeDtypeStruct((m, d), BF16)]
    if with_transpose:
        out_specs.append(pl.BlockSpec((d, tm), lambda i: (0, i)))
        out_shape.append(jax.ShapeDtypeStruct((d, m), BF16))
    out = pl.pallas_call(
        _rmsnorm_kernel,
        grid=(m // tm,),
        in_specs=[pl.BlockSpec((tm, d), lambda i: (i, 0)),
                  pl.BlockSpec((1, d), lambda i: (0, 0))],
        out_specs=out_specs,
        out_shape=out_shape,
        compiler_params=_params(("parallel",), V7X_VMEM_BUDGET),
        name="rmsnorm_bf16",
    )(x, g.reshape(1, d))
    return out if with_transpose else out[0]


def _proj_kernel(u_ref, w_ref, gain_ref, rc_ref, rs_ref, o_ref, *, classes, first_valid_row):
    j = pl.program_id(1)
    acc = jnp.dot(u_ref[...], w_ref[...], preferred_element_type=F32)
    tm, tn = acc.shape
    groups = [slice(g * V7X_LANES, (g + 1) * V7X_LANES) for g in range(tn // V7X_LANES)]

    def emit(kind):
        if kind == "headnorm":
            for sl in groups:
                x = acc[:, sl]
                ms = jnp.mean(x * x, axis=-1, keepdims=True)
                o_ref[:, sl] = (x * lax.rsqrt(ms + EPS) * gain_ref[:, sl]).astype(o_ref.dtype)
        elif kind == "rotary":
            c = rc_ref[...]
            s = rs_ref[...]
            for sl in groups:
                x = acc[:, sl]
                o_ref[:, sl] = (x * c + pltpu.roll(x, V7X_LANES // 2, 1) * s).astype(o_ref.dtype)
        elif kind == "plain":
            o_ref[...] = acc.astype(o_ref.dtype)
        elif kind == "vmasked":
            if first_valid_row > 0:
                rows = lax.broadcasted_iota(jnp.int32, (tm, 1), 0) + pl.program_id(0) * tm
                o_ref[...] = jnp.where(rows >= first_valid_row, acc, 0.0).astype(o_ref.dtype)
            else:
                o_ref[...] = acc.astype(o_ref.dtype)
        elif kind == "silu":
            o_ref[...] = (acc * _sigmoid(acc)).astype(o_ref.dtype)
        elif kind == "sigmoid":
            o_ref[...] = _sigmoid(acc).astype(o_ref.dtype)
        else:
            raise ValueError(kind)

    for kind, lo, hi in classes:
        pl.when((j >= lo) & (j < hi))(functools.partial(emit, kind))


def _proj(u, wp, gain, rot_c, rot_s, *, tm, tn, d_model, first_valid_row):
    m, d = u.shape
    npk = wp.shape[1]
    n_row_tiles = rot_c.shape[1] // tm
    widths = (("headnorm", 2 * FOX_W), ("rotary", 2 * RET_QK_W), ("plain", FOX_W),
              ("vmasked", RET_V_W), ("silu", RET_V_W), ("sigmoid", 2 * d_model))
    classes, lo = [], 0
    for kind, w in widths:
        classes.append((kind, lo, lo + w // tn))
        lo += w // tn
    assert lo * tn == npk
    rk_lo = Z_RK // tn
    kern = functools.partial(_proj_kernel, classes=tuple(classes), first_valid_row=first_valid_row)
    rot_map = lambda i, j: ((j >= rk_lo).astype(jnp.int32), i % n_row_tiles, 0)
    return pl.pallas_call(
        kern,
        grid=(m // tm, npk // tn),
        in_specs=[pl.BlockSpec((tm, d), lambda i, j: (i, 0)),
                  pl.BlockSpec((d, tn), lambda i, j: (0, j)),
                  pl.BlockSpec((1, tn), lambda i, j: (0, j)),
                  pl.BlockSpec((None, tm, V7X_LANES), rot_map),
                  pl.BlockSpec((None, tm, V7X_LANES), rot_map)],
        out_specs=pl.BlockSpec((tm, tn), lambda i, j: (i, j)),
        out_shape=jax.ShapeDtypeStruct((m, npk), BF16),
        compiler_params=_params(("parallel", "arbitrary"), V7X_VMEM_BUDGET),
        name="input_proj",
    )(u, wp, gain, rot_c, rot_s)


def _split3(x):
    hi = x.astype(BF16)
    r = x - hi.astype(F32)
    mid = r.astype(BF16)
    lo = (r - mid.astype(F32)).astype(BF16)
    return hi, mid, lo


def _forget_kernel(u_ref, w_ref, b_ref, cin_ref, selq_ref, selk_ref, o_ref, cqx_ref, ckx_ref, carry_ref, *,
                   first_valid_row):
    i = pl.program_id(1)

    @pl.when(i == 0)
    def _():
        carry_ref[...] = cin_ref[...]

    x = jnp.dot(u_ref[...], w_ref[...], preferred_element_type=F32) + b_ref[...]
    tm = x.shape[0]
    logf = jnp.minimum(x, 0.0) - jnp.log1p(jnp.exp(-jnp.abs(x)))
    rows = lax.broadcasted_iota(jnp.int32, (tm, tm), 0)
    cols = lax.broadcasted_iota(jnp.int32, (tm, tm), 1)
    if first_valid_row > 0:
        r1 = lax.broadcasted_iota(jnp.int32, (tm, 1), 0) + i * tm
        logf = jnp.where(r1 >= first_valid_row, logf, 0.0)
    tri = jnp.where(rows >= cols, 1.0, 0.0).astype(BF16)
    hi, mid, lo = _split3(logf)
    cs = (jnp.dot(tri, hi, preferred_element_type=F32)
          + jnp.dot(tri, mid, preferred_element_type=F32)
          + jnp.dot(tri, lo, preferred_element_type=F32))
    c = cs + carry_ref[...]
    o_ref[...] = c
    carry_ref[...] = c[tm - 1:tm, :]
    chi, cmid, clo = _split3(c * LOG2_E)
    lane = lax.broadcasted_iota(jnp.int32, c.shape, 1)
    one = jnp.where(lane < FOX_HEADS, 1.0, 0.0).astype(BF16)
    pq = jnp.concatenate([chi, cmid, clo, one, one, one], axis=1)
    pk = jnp.concatenate([one, one, one, -chi, -cmid, -clo], axis=1)
    cqx_ref[...] = jnp.dot(pq, selq_ref[...], preferred_element_type=F32).astype(BF16)
    ckx_ref[...] = jnp.dot(pk, selk_ref[...], preferred_element_type=F32).astype(BF16)


FOX_EXT = 6


def _forget_selectors():
    import numpy as np
    selq = np.zeros((FOX_EXT * V7X_LANES, V7X_LANES), np.float32)
    selk = np.zeros((FOX_EXT * V7X_LANES, FOX_W), np.float32)
    for p in range(FOX_EXT):
        for h in range(FOX_HEADS):
            selq[p * V7X_LANES + h, FOX_EXT * h + p] = 1.0
            selk[p * V7X_LANES + h, FOX_HD * h + FOX_EXT * h + p] = 1.0
    return jnp.asarray(selq, BF16), jnp.asarray(selk, BF16)


def _forget_cumsum(u, wf, bf, carry_in, *, nb, tm, first_valid_row):
    m, d = u.shape
    nt = m // (nb * tm)
    selq, selk = _forget_selectors()
    kern = functools.partial(_forget_kernel, first_valid_row=first_valid_row)
    const = lambda b, i: (0, 0)
    row = lambda b, i: (b * nt + i, 0)
    return pl.pallas_call(
        kern,
        grid=(nb, nt),
        in_specs=[pl.BlockSpec((tm, d), row),
                  pl.BlockSpec((d, V7X_LANES), const),
                  pl.BlockSpec((1, V7X_LANES), const),
                  pl.BlockSpec((1, V7X_LANES), const),
                  pl.BlockSpec(selq.shape, const),
                  pl.BlockSpec(selk.shape, const)],
        out_specs=[pl.BlockSpec((tm, V7X_LANES), row),
                   pl.BlockSpec((tm, V7X_LANES), row),
                   pl.BlockSpec((tm, FOX_W), row)],
        out_shape=[jax.ShapeDtypeStruct((m, V7X_LANES), F32),
                   jax.ShapeDtypeStruct((m, V7X_LANES), BF16),
                   jax.ShapeDtypeStruct((m, FOX_W), BF16)],
        scratch_shapes=[pltpu.VMEM((1, V7X_LANES), F32)],
        compiler_params=_params(("parallel", "arbitrary"), V7X_VMEM_BUDGET),
        name="forget_cumsum",
    )(u, wf, bf, carry_in, selq, selk)


def _fox_kernel(qi_ref, kj_ref, kind_ref, q_ref, cqx_ref, km_ref, vm_ref, ckm_ref, k_ref, v_ref,
                ck_ref, o_ref, m_sc, acc_sc, qx_sc, *, heads):
    t = pl.program_id(2)
    kind = kind_ref[t]
    hd = FOX_HD
    lanes = [slice(g * hd, (g + 1) * hd) for g in range(heads)]

    def scores(g, k_r, ck_r):
        sl = lanes[g]
        k_ext = jnp.concatenate([k_r[:, sl], ck_r[:, sl]], axis=1)
        return _nt(k_ext, qx_sc[g])

    def pv(p, v):
        ones = jnp.ones((16, v.shape[0]), v.dtype)
        return jnp.dot(jnp.concatenate([v.T, ones], axis=0), p, preferred_element_type=F32)

    def update(g, s, v):
        m_prev = m_sc[g]
        m_new = jnp.maximum(m_prev, jnp.max(s, axis=0, keepdims=True))
        alpha = jnp.exp2(m_prev - m_new)
        p = jnp.exp2(s - m_new).astype(BF16)
        acc_sc[g] = alpha * acc_sc[g] + pv(p, v)
        m_sc[g] = m_new

    @pl.when(kind == 0)
    def _():
        for g, sl in enumerate(lanes):
            qx_sc[g] = jnp.concatenate([q_ref[:, sl], cqx_ref[...]], axis=1)
            s = scores(g, km_ref, ckm_ref)
            kpos = lax.broadcasted_iota(jnp.int32, s.shape, 0)
            s = jnp.where(kpos >= PAD, s, NEG)
            m = jnp.max(s, axis=0, keepdims=True)
            p = jnp.exp2(s - m).astype(BF16)
            m_sc[g] = m
            acc_sc[g] = pv(p, vm_ref[:, sl])

    @pl.when(kind == 1)
    def _():
        for g, sl in enumerate(lanes):
            update(g, scores(g, k_ref, ck_ref), v_ref[:, sl])

    @pl.when(kind == 2)
    def _():
        for g, sl in enumerate(lanes):
            s = scores(g, k_ref, ck_ref)
            krow = lax.broadcasted_iota(jnp.int32, s.shape, 0)
            qcol = lax.broadcasted_iota(jnp.int32, s.shape, 1)
            s = jnp.where(krow <= qcol, s, NEG)
            update(g, s, v_ref[:, sl])
            acc = acc_sc[g]
            o_ref[:, sl] = (acc[:hd, :] / acc[hd:hd + 1, :]).T.astype(o_ref.dtype)


def _fox_attention(z3, zm, cqx, ckx, ckxm, *, tq, heads):
    b, s, _ = z3.shape
    nq = s // tq
    qi, kj, kind = [], [], []
    for i in range(nq):
        qi += [i] * (i + 2)
        kj += [0] + list(range(i + 1))
        kind += [0] + [1] * i + [2]
    n_steps = len(qi)
    qi = jnp.asarray(qi, jnp.int32)
    kj = jnp.asarray(kj, jnp.int32)
    kind = jnp.asarray(kind, jnp.int32)
    w = heads * FOX_HD
    fq, fk, fv = Z_FQ // w, Z_FK // w, Z_FV // w
    grid_spec = pltpu.PrefetchScalarGridSpec(
        num_scalar_prefetch=3,
        grid=(b, FOX_HEADS // heads, n_steps),
        in_specs=[
            pl.BlockSpec((None, tq, w), lambda b_, h, t, qi, kj, kd: (b_, qi[t], fq + h)),
            pl.BlockSpec((None, tq, V7X_LANES), lambda b_, h, t, qi, kj, kd: (b_, qi[t], 0)),
            pl.BlockSpec((BLOCK, w), lambda b_, h, t, qi, kj, kd: (0, fk + h)),
            pl.BlockSpec((BLOCK, w), lambda b_, h, t, qi, kj, kd: (0, fv + h)),
            pl.BlockSpec((BLOCK, w), lambda b_, h, t, qi, kj, kd: (0, h)),
            pl.BlockSpec((None, tq, w), lambda b_, h, t, qi, kj, kd: (b_, kj[t], fk + h)),
            pl.BlockSpec((None, tq, w), lambda b_, h, t, qi, kj, kd: (b_, kj[t], fv + h)),
            pl.BlockSpec((None, tq, w), lambda b_, h, t, qi, kj, kd: (b_, kj[t], h)),
        ],
        out_specs=pl.BlockSpec((None, tq, w), lambda b_, h, t, qi, kj, kd: (b_, qi[t], h)),
        scratch_shapes=[pltpu.VMEM((heads, 1, tq), F32),
                        pltpu.VMEM((heads, FOX_HD + 16, tq), F32),
                        pltpu.VMEM((heads, tq, 2 * FOX_HD), BF16)],
    )
    return pl.pallas_call(
        functools.partial(_fox_kernel, heads=heads),
        grid_spec=grid_spec,
        out_shape=jax.ShapeDtypeStruct((b, s, FOX_W), BF16),
        compiler_params=_params(("parallel", "parallel", "arbitrary")),
        name="fox_attention",
    )(qi, kj, kind, z3, cqx, zm, zm, ckxm, z3, z3, ckx)


def _ret_kernel(q_ref, k_ref, v_ref, g_ref, km_ref, vm_ref, gn_ref, o_ref, state_sc, decay_sc,
                xi_sc, zeta_sc, cd_sc, *, heads):
    hg = pl.program_id(1)
    c = pl.program_id(2)
    chunk = q_ref.shape[0]
    dk, dv = RET_DK, RET_DV

    def log_gamma(head, width):
        hv = jnp.full((1, width), head, jnp.int32).astype(F32)
        return jnp.log1p(-jnp.exp2(-5.0 - hv))

    def rows(n, width):
        return lax.broadcasted_iota(jnp.int32, (n, width), 0).astype(F32)

    @pl.when(c == 0)
    def _():
        for g in range(heads):
            head = hg * heads + g
            diff = rows(chunk, chunk) - lax.broadcasted_iota(jnp.int32, (chunk, chunk), 1).astype(F32)
            decay_sc[g] = jnp.where(diff >= 0, jnp.exp(log_gamma(head, chunk) * jnp.maximum(diff, 0.0)), 0.0)
            xi_sc[g] = jnp.exp(log_gamma(head, dv) * (rows(chunk, dv) + 1.0))
            zeta_sc[g] = jnp.exp(log_gamma(head, dk) * (chunk - 1.0 - rows(chunk, dk)))
            cd_sc[g] = jnp.exp(log_gamma(head, dv) * float(chunk))
            zeta_m = jnp.exp(log_gamma(head, dk) * (BLOCK - 1.0 - rows(BLOCK, dk)))
            kz = (km_ref[:, g * dk:(g + 1) * dk].astype(F32) * zeta_m).astype(BF16)
            state_sc[g] = _tn(kz, vm_ref[:, g * dv:(g + 1) * dv])

    for g in range(heads):
        q = q_ref[:, g * dk:(g + 1) * dk]
        k = k_ref[:, g * dk:(g + 1) * dk]
        v = v_ref[:, g * dv:(g + 1) * dv]
        s = _nt(q, k) * decay_sc[g]
        intra = jnp.dot(s.astype(BF16), v, preferred_element_type=F32)
        state = state_sc[g]
        cross = jnp.dot(q, state.astype(BF16), preferred_element_type=F32) * xi_sc[g]
        o = intra + cross
        kz = (k.astype(F32) * zeta_sc[g]).astype(BF16)
        state_sc[g] = state * cd_sc[g] + _tn(kz, v)
        mu = jnp.mean(o, axis=-1, keepdims=True)
        d = o - mu
        var = jnp.mean(d * d, axis=-1, keepdims=True)
        y = d * lax.rsqrt(var + EPS) * gn_ref[:, g * dv:(g + 1) * dv]
        o_ref[:, g * dv:(g + 1) * dv] = (g_ref[:, g * dv:(g + 1) * dv].astype(F32) * y).astype(o_ref.dtype)


def _retention(z3, zm, ret_norm_g, *, chunk, heads):
    b, s, _ = z3.shape
    nc = s // chunk
    wk, wv = heads * RET_DK, heads * RET_DV
    rq, rk = Z_RQ // wk, Z_RK // wk
    rv, gr = Z_RV // wv, Z_GR // wv
    return pl.pallas_call(
        functools.partial(_ret_kernel, heads=heads),
        grid=(b, RET_HEADS // heads, nc),
        in_specs=[
            pl.BlockSpec((None, chunk, wk), lambda b_, h, c: (b_, c, rq + h)),
            pl.BlockSpec((None, chunk, wk), lambda b_, h, c: (b_, c, rk + h)),
            pl.BlockSpec((None, chunk, wv), lambda b_, h, c: (b_, c, rv + h)),
            pl.BlockSpec((None, chunk, wv), lambda b_, h, c: (b_, c, gr + h)),
            pl.BlockSpec((BLOCK, wk), lambda b_, h, c: (0, rk + h)),
            pl.BlockSpec((BLOCK, wv), lambda b_, h, c: (0, rv + h)),
            pl.BlockSpec((1, wv), lambda b_, h, c: (0, h)),
        ],
        out_specs=pl.BlockSpec((None, chunk, wv), lambda b_, h, c: (b_, c, h)),
        out_shape=jax.ShapeDtypeStruct((b, s, RET_V_W), BF16),
        scratch_shapes=[pltpu.VMEM((heads, RET_DK, RET_DV), F32),
                        pltpu.VMEM((heads, chunk, chunk), F32),
                        pltpu.VMEM((heads, chunk, RET_DV), F32),
                        pltpu.VMEM((heads, chunk, RET_DK), F32),
                        pltpu.VMEM((heads, 1, RET_DV), F32)],
        compiler_params=_params(("parallel", "parallel", "arbitrary")),
        name="retention",
    )(z3, z3, z3, z3, zm, zm, ret_norm_g.reshape(1, RET_V_W))


def _merge_kernel(ya_ref, yr_ref, wf_ref, wr_ref, ga_ref, gr_ref, o_ref):
    a = jnp.dot(ya_ref[...], wf_ref[...], preferred_element_type=F32)
    r = jnp.dot(yr_ref[...], wr_ref[...], preferred_element_type=F32)
    o_ref[...] = (ga_ref[...].astype(F32) * a + gr_ref[...].astype(F32) * r).astype(o_ref.dtype)


def _merge(ya, yr, wf, wr, z, *, tm, tn, d_model):
    m = ya.shape[0]
    ga0 = Z_GA // tn
    gr0 = (Z_GA + d_model) // tn
    return pl.pallas_call(
        _merge_kernel,
        grid=(m // tm, d_model // tn),
        in_specs=[pl.BlockSpec((tm, FOX_W), lambda i, j: (i, 0)),
                  pl.BlockSpec((tm, RET_V_W), lambda i, j: (i, 0)),
                  pl.BlockSpec((FOX_W, tn), lambda i, j: (0, j)),
                  pl.BlockSpec((RET_V_W, tn), lambda i, j: (0, j)),
                  pl.BlockSpec((tm, tn), lambda i, j: (i, ga0 + j)),
                  pl.BlockSpec((tm, tn), lambda i, j: (i, gr0 + j))],
        out_specs=pl.BlockSpec((tm, tn), lambda i, j: (i, j)),
        out_shape=jax.ShapeDtypeStruct((m, d_model), BF16),
        compiler_params=_params(("parallel", "arbitrary"), V7X_VMEM_BUDGET),
        name="merge_proj",
    )(ya, yr, wf, wr, z, z)


def _outproj_kernel(a_ref, w_ref, x_ref, o_ref):
    o_ref[...] = x_ref[...] + jnp.dot(a_ref[...], w_ref[...], preferred_element_type=F32)


def _outproj(a, w, x, *, tm, tn):
    m, k = a.shape
    n = w.shape[1]
    return pl.pallas_call(
        _outproj_kernel,
        grid=(m // tm, n // tn),
        in_specs=[pl.BlockSpec((tm, k), lambda i, j: (i, 0)),
                  pl.BlockSpec((k, tn), lambda i, j: (0, j)),
                  pl.BlockSpec((tm, tn), lambda i, j: (i, j))],
        out_specs=pl.BlockSpec((tm, tn), lambda i, j: (i, j)),
        out_shape=jax.ShapeDtypeStruct((m, n), F32),
        compiler_params=_params(("parallel", "arbitrary"), V7X_VMEM_BUDGET),
        name="out_proj",
    )(a, w, x)


def _peer_score_kernel(x_ref, w_ref, keys_ref, o_ref):
    acc = jnp.dot(x_ref[...], w_ref[...], preferred_element_type=F32)
    for g in range(acc.shape[1] // PEER_HALF):
        qg = acc[:, g * PEER_HALF:(g + 1) * PEER_HALF].astype(BF16)
        o_ref[g] = _nt(keys_ref[g], qg)


def _peer_scores(xn, wq, keys, *, tm, tn):
    m, d = xn.shape
    n = wq.shape[1]
    ng = tn // PEER_HALF
    return pl.pallas_call(
        _peer_score_kernel,
        grid=(m // tm, n // tn),
        in_specs=[pl.BlockSpec((tm, d), lambda i, j: (i, 0)),
                  pl.BlockSpec((d, tn), lambda i, j: (0, j)),
                  pl.BlockSpec((ng, PEER_NKEYS, PEER_HALF), lambda i, j: (j, 0, 0))],
        out_specs=pl.BlockSpec((ng, PEER_NKEYS, tm), lambda i, j: (j, 0, i)),
        out_shape=jax.ShapeDtypeStruct((n // PEER_HALF, PEER_NKEYS, m), F32),
        compiler_params=_params(("parallel", "arbitrary"), V7X_VMEM_BUDGET),
        name="peer_scores",
    )(xn, wq, keys)


def _top_values(x, n, stacked_rows):
    rows, width = x.shape
    idx = lax.broadcasted_iota(jnp.int32, (rows, width), 0).astype(F32)
    srow = lax.broadcasted_iota(jnp.int32, (stacked_rows, width), 0)
    stacked = jnp.zeros((stacked_rows, width), F32)
    rank = jnp.full((rows, width), float(n), F32)
    vals = []
    for t in range(n):
        cur = jnp.max(x, axis=0, keepdims=True)
        vals.append(cur)
        stacked = jnp.where(srow == t, cur, stacked)
        first = jnp.min(jnp.where(x == cur, idx, float(rows)), axis=0, keepdims=True)
        hit = idx == first
        rank = jnp.where(hit, float(t), rank)
        x = jnp.where(hit, -jnp.inf, x)
    return stacked, vals, rank


def _peer_topk_kernel(s_ref, e1_ref, e2_ref, rank_ref, cnt_ref):
    s1 = s_ref[0]
    s2 = s_ref[1]
    v1, v1l, rank1 = _top_values(s1, PEER_TOPK, PEER_TOPK)
    v2, v2l, _ = _top_values(s2, PEER_TOPK, PEER_TOPK)
    sub = lax.broadcasted_iota(jnp.int32, (8, s1.shape[1]), 0)
    pieces = [v1l[0] + v2]
    for a in range(1, 8):
        nb = PEER_TOPK // (a + 1)
        pieces.append(jnp.where(sub < nb, v1l[a] + v2[0:8], -jnp.inf))
    pieces.append(v1[8:16] + v2l[0])
    cand = jnp.concatenate(pieces, axis=0)
    _, top, _ = _top_values(cand, PEER_TOPK, 8)
    m = top[0]
    z = jnp.zeros_like(m)
    for t in range(PEER_TOPK):
        z = z + jnp.exp(top[t] - m)
    tau = top[PEER_TOPK - 1]
    cnt = jnp.zeros_like(s2)
    for a in range(PEER_TOPK):
        cnt = cnt + jnp.where(v1l[a] + s2 >= tau, 1.0, 0.0)
    rank_ref[...] = rank1
    cnt_ref[...] = cnt.astype(cnt_ref.dtype)
    e1_ref[...] = jnp.exp(s1 - v1l[0])
    e2_ref[...] = (jnp.exp(s2 - v2l[0]) / z).astype(e2_ref.dtype)


def _peer_topk(st, *, tt):
    _, _, t = st.shape
    spec = pl.BlockSpec((None, PEER_NKEYS, tt), lambda i, h: (h, 0, i))
    shape = lambda dt: jax.ShapeDtypeStruct((PEER_HEADS, PEER_NKEYS, t), dt)
    return pl.pallas_call(
        _peer_topk_kernel,
        grid=(t // tt, PEER_HEADS),
        in_specs=[pl.BlockSpec((2, PEER_NKEYS, tt), lambda i, h: (h, 0, i))],
        out_specs=[spec, spec, spec, spec],
        out_shape=[shape(F32), shape(BF16), shape(F32), shape(BF16)],
        compiler_params=_params(("parallel", "parallel")),
        name="peer_topk",
    )(st)


def _gelu_exact(x):
    return 0.5 * x * (1.0 + lax.erf(x * (2.0 ** -0.5)))


def _peer_dense_kernel(xt_ref, u_ref, v_ref, e1_ref, rank_ref, e2_ref, cnt_ref, h_ref, o_ref, gate_sc, *,
                       na, halves):
    j = pl.program_id(1)

    def build_gate(chunk, gate_sc):
        sub = 16
        bshape = (sub, V7X_LANES)
        for a in range(na):
            row = chunk * na + a
            rkrows = [rank_ref[h, pl.ds(row, 1), :] for h in range(PEER_HEADS)]
            e1rows = [e1_ref[h, pl.ds(row, 1), :] for h in range(PEER_HEADS)]
            for l0 in range(0, gate_sc.shape[1], V7X_LANES):
                ls = slice(l0, l0 + V7X_LANES)
                rkb = [jnp.broadcast_to(rkrows[h][:, ls], bshape).astype(BF16) for h in range(PEER_HEADS)]
                e1b = [jnp.broadcast_to(e1rows[h][:, ls], bshape).astype(BF16) for h in range(PEER_HEADS)]
                for r in range(0, PEER_NKEYS, sub):
                    gate = None
                    for h in range(PEER_HEADS):
                        sel = rkb[h] < cnt_ref[h, r:r + sub, ls]
                        term = jnp.where(sel, e1b[h] * e2_ref[h, r:r + sub, ls], jnp.zeros(bshape, BF16))
                        gate = term if gate is None else gate + term
                    gate_sc[a * PEER_NKEYS + r:a * PEER_NKEYS + r + sub, ls] = gate

    @pl.when(j == 0)
    def _():
        o_ref[...] = h_ref[...]

    build_gate(j, gate_sc)
    rows = na * PEER_NKEYS // halves
    total = None
    for k in range(halves):
        sl = slice(k * rows, (k + 1) * rows)
        act = jnp.dot(u_ref[sl, :], xt_ref[...], preferred_element_type=F32)
        w = gate_sc[sl, :] * _gelu_exact(act).astype(BF16)
        part = _tn(w, v_ref[sl, :])
        total = part if total is None else total + part
    o_ref[...] += total


def _peer_dense(xnt, u_tab, v_tab, e1, rank1, e2, cnt, h, *, tt, ec):
    d, t = xnt.shape
    n_exp = u_tab.shape[0]
    na = ec // PEER_NKEYS
    once = pl.Buffered(1)
    kern = functools.partial(_peer_dense_kernel, na=na, halves=2)
    table = pl.BlockSpec((PEER_HEADS, PEER_NKEYS, tt), lambda i, j: (0, 0, i), pipeline_mode=once)
    return pl.pallas_call(
        kern,
        grid=(t // tt, n_exp // ec),
        in_specs=[pl.BlockSpec((d, tt), lambda i, j: (0, i), pipeline_mode=once),
                  pl.BlockSpec((ec, d), lambda i, j: (j, 0)),
                  pl.BlockSpec((ec, d), lambda i, j: (j, 0)),
                  table, table, table, table,
                  pl.BlockSpec((tt, d), lambda i, j: (i, 0), pipeline_mode=once)],
        out_specs=pl.BlockSpec((tt, d), lambda i, j: (i, 0)),
        out_shape=jax.ShapeDtypeStruct((t, d), F32),
        scratch_shapes=[pltpu.VMEM((ec, tt), BF16)],
        compiler_params=_params(("parallel", "arbitrary"), V7X_VMEM_BYTES - 4 * 1024 * 1024),
        name="peer_dense",
    )(xnt, u_tab, v_tab, e1, rank1, e2, cnt, h)


def _rotary_tables(pos, valid):
    half = RET_DK // 2
    inv = ROPE_BASE ** (-jnp.arange(half, dtype=F32) / half)
    ang = pos[:, None] * inv[None, :]
    cos, sin = jnp.cos(ang), jnp.sin(ang)
    c = jnp.concatenate([cos, cos], axis=-1)
    s = jnp.concatenate([-sin, sin], axis=-1)
    kscale = (RET_DK ** -0.5) * valid.astype(F32)[:, None]
    return jnp.stack([c, c * kscale]), jnp.stack([s, s * kscale])


def _tile(n, pref):
    t = min(n, pref)
    assert n % t == 0
    return t


def kernel(x, meta_tokens, norm_mix_g, w_in, b_forget, q_norm_g, k_norm_g, ret_norm_g, w_proj_fox,
           w_proj_ret, w_out, norm_ffn_g, peer_w_q, peer_keys_1, peer_keys_2, peer_u, peer_v):
    b, s, d = x.shape
    depth = w_in.shape[0]
    assert depth == 1, "only the real-token outputs of a single layer are computed"
    m = b * s
    xr = x.reshape(m, d)
    hm = jnp.concatenate([jnp.zeros((PAD, d), x.dtype), meta_tokens.astype(x.dtype)], axis=0)

    w = w_in[0]
    o_fq, o_fk, o_fv, o_f = 0, FOX_W, 2 * FOX_W, 3 * FOX_W
    o_rq = o_f + FOX_HEADS
    o_rk, o_rv = o_rq + RET_QK_W, o_rq + 2 * RET_QK_W
    o_gr = o_rv + RET_V_W
    o_ga = o_gr + RET_V_W
    o_gb = o_ga + d
    col = lambda o, n: w[:, o:o + n]
    wp = jnp.concatenate([col(o_fq, FOX_W), col(o_fk, FOX_W), col(o_rq, RET_QK_W), col(o_rk, RET_QK_W),
                          col(o_fv, FOX_W), col(o_rv, RET_V_W), col(o_gr, RET_V_W), col(o_ga, d),
                          col(o_gb, d)], axis=1).astype(BF16)
    npk = wp.shape[1]
    wf = jnp.pad(col(o_f, FOX_HEADS), ((0, 0), (0, V7X_LANES - FOX_HEADS))).astype(BF16)
    bf = jnp.pad(b_forget[0], (0, V7X_LANES - FOX_HEADS)).reshape(1, V7X_LANES)
    gain = jnp.concatenate([jnp.tile(q_norm_g[0] * (FOX_HD ** -0.5 * LOG2_E), FOX_HEADS),
                            jnp.tile(k_norm_g[0], FOX_HEADS),
                            jnp.zeros((npk - 2 * FOX_W,), F32)]).reshape(1, npk)

    pos_real = jnp.arange(s, dtype=F32) + N_META
    rc, rs = _rotary_tables(pos_real, jnp.ones((s,), bool))
    pos_meta = jnp.arange(BLOCK, dtype=F32) - PAD
    rcm, rsm = _rotary_tables(pos_meta, jnp.arange(BLOCK) >= PAD)

    tm = _tile(m, 1024)
    tn = _tile(d, 1024)

    u = _rmsnorm_bf16(xr, norm_mix_g[0], _tile(m, 512))
    um = _rmsnorm_bf16(hm, norm_mix_g[0], BLOCK)
    z = _proj(u, wp, gain, rc, rs, tm=_tile(s, 1024), tn=tn, d_model=d, first_valid_row=0)
    zm = _proj(um, wp, gain, rcm, rsm, tm=BLOCK, tn=tn, d_model=d, first_valid_row=PAD)
    cm, _, ckxm = _forget_cumsum(um, wf, bf, jnp.zeros((1, V7X_LANES), F32), nb=1, tm=BLOCK,
                                 first_valid_row=PAD)
    _, cqx, ckx = _forget_cumsum(u, wf, bf, cm[BLOCK - 1:BLOCK], nb=b, tm=_tile(s, 512), first_valid_row=0)

    z3 = z.reshape(b, s, npk)
    ya = _fox_attention(z3, zm, cqx.reshape(b, s, V7X_LANES), ckx.reshape(b, s, FOX_W), ckxm,
                        tq=_tile(s, 512), heads=8)
    chunk = _tile(s, 256)
    yr = _retention(z3, zm, ret_norm_g[0], chunk=chunk, heads=4)

    merged = _merge(ya.reshape(m, FOX_W), yr.reshape(m, RET_V_W), w_proj_fox[0].astype(BF16),
                    w_proj_ret[0].astype(BF16), z, tm=tm, tn=_tile(d, 512), d_model=d)
    h1 = _outproj(merged, w_out[0].astype(BF16), xr, tm=tm, tn=tn)

    xn, xnt = _rmsnorm_bf16(h1, norm_ffn_g[0], _tile(m, 512), with_transpose=True)
    keys = jnp.stack([peer_keys_1[0], peer_keys_2[0]], axis=1).reshape(
        2 * PEER_HEADS, PEER_NKEYS, PEER_HALF).astype(BF16)
    st = _peer_scores(xn, peer_w_q[0].astype(BF16), keys, tm=tm, tn=_tile(PEER_HEADS * PEER_QDIM, 1024))
    e1, e2, rank1, cnt = _peer_topk(st, tt=_tile(m, 512))
    out = _peer_dense(xnt, peer_u[0].astype(BF16), peer_v[0].astype(BF16), e1, rank1, e2, cnt, h1,
                      tt=_tile(m, 512), ec=512)
    return out.reshape(b, s, d)
```

```python
import functools

import jax
import jax.numpy as jnp
from jax import lax
from jax.experimental import pallas as pl
from jax.experimental.pallas import tpu as pltpu

N_META = 16
BLOCK = 128
PAD = BLOCK - N_META
EPS = 1e-6
NEG = -1e30
LOG2_E = 1.4426950408889634
FOX_HEADS = 16
FOX_HD = 128
FOX_W = FOX_HEADS * FOX_HD
RET_HEADS = 16
RET_DK = 128
RET_DV = 256
RET_QK_W = RET_HEADS * RET_DK
RET_V_W = RET_HEADS * RET_DV
ROPE_BASE = 10000.0
PEER_HEADS = 8
PEER_NKEYS = 128
PEER_N = PEER_NKEYS * PEER_NKEYS
PEER_QDIM = 256
PEER_HALF = PEER_QDIM // 2
PEER_TOPK = 16

V7X_LANES = 128
V7X_VMEM_BYTES = 64 * 1024 * 1024
V7X_VMEM_BUDGET = 56 * 1024 * 1024

BF16 = jnp.bfloat16
F32 = jnp.float32

Z_FQ = 0
Z_FK = Z_FQ + FOX_W
Z_RQ = Z_FK + FOX_W
Z_RK = Z_RQ + RET_QK_W
Z_FV = Z_RK + RET_QK_W
Z_RV = Z_FV + FOX_W
Z_GR = Z_RV + RET_V_W
Z_GA = Z_GR + RET_V_W


def _params(sem, vmem=None):
    return pltpu.CompilerParams(dimension_semantics=sem, vmem_limit_bytes=vmem)


def _nt(a, b):
    return lax.dot_general(a, b, (((1,), (1,)), ((), ())), preferred_element_type=F32)


def _tn(a, b):
    return lax.dot_general(a, b, (((0,), (0,)), ((), ())), preferred_element_type=F32)


def _sigmoid(x):
    return 0.5 + 0.5 * jnp.tanh(0.5 * x)


def _rmsnorm_kernel(x_ref, g_ref, o_ref, *ot_ref):
    x = x_ref[...]
    ms = jnp.mean(x * x, axis=-1, keepdims=True)
    y = (x * lax.rsqrt(ms + EPS) * g_ref[...]).astype(o_ref.dtype)
    o_ref[...] = y
    if ot_ref:
        ot_ref[0][...] = y.T


def _rmsnorm_bf16(x, g, tm, with_transpose=False):
    m, d = x.shape
    out_specs = [pl.BlockSpec((tm, d), lambda i: (i, 0))]
    out_shape = [jax.ShapeDtypeStruct((m, d), BF16)]
    if with_transpose:
        out_specs.append(pl.BlockSpec((d, tm), lambda i: (0, i)))
        out_shape.append(jax.ShapeDtypeStruct((d, m), BF16))
    out = pl.pallas_call(
        _rmsnorm_kernel,
        grid=(m // tm,),
        in_specs=[pl.BlockSpec((tm, d), lambda i: (i, 0)),
                  pl.BlockSpec((1, d), lambda i: (0, 0))],
        out_specs=out_specs,
        out_shape=out_shape,
        compiler_params=_params(("parallel",), V7X_VMEM_BUDGET),
        name="rmsnorm_bf16",
    )(x, g.reshape(1, d))
    return out if with_transpose else out[0]


def _proj_kernel(u_ref, w_ref, gain_ref, rc_ref, rs_ref, o_ref, *, classes, first_valid_row):
    j = pl.program_id(1)
    acc = jnp.dot(u_ref[...], w_ref[...], preferred_element_type=F32)
    tm, tn = acc.shape
    groups = [slice(g * V7X_LANES, (g + 1) * V7X_LANES) for g in range(tn // V7X_LANES)]

    def emit(kind):
        if kind == "headnorm":
            for sl in groups:
                x = acc[:, sl]
                ms = jnp.mean(x * x, axis=-1, keepdims=True)
                o_ref[:, sl] = (x * lax.rsqrt(ms + EPS) * gain_ref[:, sl]).astype(o_ref.dtype)
        elif kind == "rotary":
            c = rc_ref[...]
            s = rs_ref[...]
            for sl in groups:
                x = acc[:, sl]
                o_ref[:, sl] = (x * c + pltpu.roll(x, V7X_LANES // 2, 1) * s).astype(o_ref.dtype)
        elif kind == "plain":
            o_ref[...] = acc.astype(o_ref.dtype)
        elif kind == "vmasked":
            if first_valid_row > 0:
                rows = lax.broadcasted_iota(jnp.int32, (tm, 1), 0) + pl.program_id(0) * tm
                o_ref[...] = jnp.where(rows >= first_valid_row, acc, 0.0).astype(o_ref.dtype)
            else:
                o_ref[...] = acc.astype(o_ref.dtype)
        elif kind == "silu":
            o_ref[...] = (acc * _sigmoid(acc)).astype(o_ref.dtype)
        elif kind == "sigmoid":
            o_ref[...] = _sigmoid(acc).astype(o_ref.dtype)
        else:
            raise ValueError(kind)

    for kind, lo, hi in classes:
        pl.when((j >= lo) & (j < hi))(functools.partial(emit, kind))


def _proj(u, wp, gain, rot_c, rot_s, *, tm, tn, d_model, first_valid_row):
    m, d = u.shape
    npk = wp.shape[1]
    n_row_tiles = rot_c.shape[1] // tm
    widths = (("headnorm", 2 * FOX_W), ("rotary", 2 * RET_QK_W), ("plain", FOX_W),
              ("vmasked", RET_V_W), ("silu", RET_V_W), ("sigmoid", 2 * d_model))
    classes, lo = [], 0
    for kind, w in widths:
        classes.append((kind, lo, lo + w // tn))
        lo += w // tn
    assert lo * tn == npk
    rk_lo = Z_RK // tn
    kern = functools.partial(_proj_kernel, classes=tuple(classes), first_valid_row=first_valid_row)
    rot_map = lambda i, j: ((j >= rk_lo).astype(jnp.int32), i % n_row_tiles, 0)
    return pl.pallas_call(
        kern,
        grid=(m // tm, npk // tn),
        in_specs=[pl.BlockSpec((tm, d), lambda i, j: (i, 0)),
                  pl.BlockSpec((d, tn), lambda i, j: (0, j)),
                  pl.BlockSpec((1, tn), lambda i, j: (0, j)),
                  pl.BlockSpec((None, tm, V7X_LANES), rot_map),
                  pl.BlockSpec((None, tm, V7X_LANES), rot_map)],
        out_specs=pl.BlockSpec((tm, tn), lambda i, j: (i, j)),
        out_shape=jax.ShapeDtypeStruct((m, npk), BF16),
        compiler_params=_params(("parallel", "arbitrary"), V7X_VMEM_BUDGET),
        name="input_proj",
    )(u, wp, gain, rot_c, rot_s)


def _split3(x):
    hi = x.astype(BF16)
    r = x - hi.astype(F32)
    mid = r.astype(BF16)
    lo = (r - mid.astype(F32)).astype(BF16)
    return hi, mid, lo


def _forget_kernel(u_ref, w_ref, b_ref, cin_ref, selq_ref, selk_ref, o_ref, cqx_ref, ckx_ref, carry_ref, *,
                   first_valid_row):
    i = pl.program_id(1)

    @pl.when(i == 0)
    def _():
        carry_ref[...] = cin_ref[...]

    x = jnp.dot(u_ref[...], w_ref[...], preferred_element_type=F32) + b_ref[...]
    tm = x.shape[0]
    logf = jnp.minimum(x, 0.0) - jnp.log1p(jnp.exp(-jnp.abs(x)))
    rows = lax.broadcasted_iota(jnp.int32, (tm, tm), 0)
    cols = lax.broadcasted_iota(jnp.int32, (tm, tm), 1)
    if first_valid_row > 0:
        r1 = lax.broadcasted_iota(jnp.int32, (tm, 1), 0) + i * tm
        logf = jnp.where(r1 >= first_valid_row, logf, 0.0)
    tri = jnp.where(rows >= cols, 1.0, 0.0).astype(BF16)
    hi, mid, lo = _split3(logf)
    cs = (jnp.dot(tri, hi, preferred_element_type=F32)
          + jnp.dot(tri, mid, preferred_element_type=F32)
          + jnp.dot(tri, lo, preferred_element_type=F32))
    c = cs + carry_ref[...]
    o_ref[...] = c
    carry_ref[...] = c[tm - 1:tm, :]
    chi, cmid, clo = _split3(c * LOG2_E)
    lane = lax.broadcasted_iota(jnp.int32, c.shape, 1)
    one = jnp.where(lane < FOX_HEADS, 1.0, 0.0).astype(BF16)
    pq = jnp.concatenate([chi, cmid, clo, one, one, one], axis=1)
    pk = jnp.concatenate([one, one, one, -chi, -cmid, -clo], axis=1)
    cqx_ref[...] = jnp.dot(pq, selq_ref[...], preferred_element_type=F32).astype(BF16)
    ckx_ref[...] = jnp.dot(pk, selk_ref[...], preferred_element_type=F32).astype(BF16)


FOX_EXT = 6


def _forget_selectors():
    import numpy as np
    selq = np.zeros((FOX_EXT * V7X_LANES, V7X_LANES), np.float32)
    selk = np.zeros((FOX_EXT * V7X_LANES, FOX_W), np.float32)
    for p in range(FOX_EXT):
        for h in range(FOX_HEADS):
            selq[p * V7X_LANES + h, FOX_EXT * h + p] = 1.0
            selk[p * V7X_LANES + h, FOX_HD * h + FOX_EXT * h + p] = 1.0
    return jnp.asarray(selq, BF16), jnp.asarray(selk, BF16)


def _forget_cumsum(u, wf, bf, carry_in, *, nb, tm, first_valid_row):
    m, d = u.shape
    nt = m // (nb * tm)
    selq, selk = _forget_selectors()
    kern = functools.partial(_forget_kernel, first_valid_row=first_valid_row)
    const = lambda b, i: (0, 0)
    row = lambda b, i: (b * nt + i, 0)
    return pl.pallas_call(
        kern,
        grid=(nb, nt),
        in_specs=[pl.BlockSpec((tm, d), row),
                  pl.BlockSpec((d, V7X_LANES), const),
                  pl.BlockSpec((1, V7X_LANES), const),
                  pl.BlockSpec((1, V7X_LANES), const),
                  pl.BlockSpec(selq.shape, const),
                  pl.BlockSpec(selk.shape, const)],
        out_specs=[pl.BlockSpec((tm, V7X_LANES), row),
                   pl.BlockSpec((tm, V7X_LANES), row),
                   pl.BlockSpec((tm, FOX_W), row)],
        out_shape=[jax.ShapeDtypeStruct((m, V7X_LANES), F32),
                   jax.ShapeDtypeStruct((m, V7X_LANES), BF16),
                   jax.ShapeDtypeStruct((m, FOX_W), BF16)],
        scratch_shapes=[pltpu.VMEM((1, V7X_LANES), F32)],
        compiler_params=_params(("parallel", "arbitrary"), V7X_VMEM_BUDGET),
        name="forget_cumsum",
    )(u, wf, bf, carry_in, selq, selk)


def _fox_kernel(qi_ref, kj_ref, kind_ref, q_ref, cqx_ref, km_ref, vm_ref, ckm_ref, k_ref, v_ref,
                ck_ref, o_ref, m_sc, acc_sc, qx_sc, *, heads):
    t = pl.program_id(2)
    kind = kind_ref[t]
    hd = FOX_HD
    lanes = [slice(g * hd, (g + 1) * hd) for g in range(heads)]

    def scores(g, k_r, ck_r):
        sl = lanes[g]
        k_ext = jnp.concatenate([k_r[:, sl], ck_r[:, sl]], axis=1)
        return _nt(k_ext, qx_sc[g])

    def pv(p, v):
        ones = jnp.ones((16, v.shape[0]), v.dtype)
        return jnp.dot(jnp.concatenate([v.T, ones], axis=0), p, preferred_element_type=F32)

    def update(g, s, v):
        m_prev = m_sc[g]
        m_new = jnp.maximum(m_prev, jnp.max(s, axis=0, keepdims=True))
        alpha = jnp.exp2(m_prev - m_new)
        p = jnp.exp2(s - m_new).astype(BF16)
        acc_sc[g] = alpha * acc_sc[g] + pv(p, v)
        m_sc[g] = m_new

    @pl.when(kind == 0)
    def _():
        for g, sl in enumerate(lanes):
            qx_sc[g] = jnp.concatenate([q_ref[:, sl], cqx_ref[...]], axis=1)
            s = scores(g, km_ref, ckm_ref)
            kpos = lax.broadcasted_iota(jnp.int32, s.shape, 0)
            s = jnp.where(kpos >= PAD, s, NEG)
            m = jnp.max(s, axis=0, keepdims=True)
            p = jnp.exp2(s - m).astype(BF16)
            m_sc[g] = m
            acc_sc[g] = pv(p, vm_ref[:, sl])

    @pl.when(kind == 1)
    def _():
        for g, sl in enumerate(lanes):
            update(g, scores(g, k_ref, ck_ref), v_ref[:, sl])

    @pl.when(kind == 2)
    def _():
        for g, sl in enumerate(lanes):
            s = scores(g, k_ref, ck_ref)
            krow = lax.broadcasted_iota(jnp.int32, s.shape, 0)
            qcol = lax.broadcasted_iota(jnp.int32, s.shape, 1)
            s = jnp.where(krow <= qcol, s, NEG)
            update(g, s, v_ref[:, sl])
            acc = acc_sc[g]
            o_ref[:, sl] = (acc[:hd, :] / acc[hd:hd + 1, :]).T.astype(o_ref.dtype)


def _fox_attention(z3, zm, cqx, ckx, ckxm, *, tq, heads):
    b, s, _ = z3.shape
    nq = s // tq
    qi, kj, kind = [], [], []
    for i in range(nq):
        qi += [i] * (i + 2)
        kj += [0] + list(range(i + 1))
        kind += [0] + [1] * i + [2]
    n_steps = len(qi)
    qi = jnp.asarray(qi, jnp.int32)
    kj = jnp.asarray(kj, jnp.int32)
    kind = jnp.asarray(kind, jnp.int32)
    w = heads * FOX_HD
    fq, fk, fv = Z_FQ // w, Z_FK // w, Z_FV // w
    grid_spec = pltpu.PrefetchScalarGridSpec(
        num_scalar_prefetch=3,
        grid=(b, FOX_HEADS // heads, n_steps),
        in_specs=[
            pl.BlockSpec((None, tq, w), lambda b_, h, t, qi, kj, kd: (b_, qi[t], fq + h)),
            pl.BlockSpec((None, tq, V7X_LANES), lambda b_, h, t, qi, kj, kd: (b_, qi[t], 0)),
            pl.BlockSpec((BLOCK, w), lambda b_, h, t, qi, kj, kd: (0, fk + h)),
            pl.BlockSpec((BLOCK, w), lambda b_, h, t, qi, kj, kd: (0, fv + h)),
            pl.BlockSpec((BLOCK, w), lambda b_, h, t, qi, kj, kd: (0, h)),
            pl.BlockSpec((None, tq, w), lambda b_, h, t, qi, kj, kd: (b_, kj[t], fk + h)),
            pl.BlockSpec((None, tq, w), lambda b_, h, t, qi, kj, kd: (b_, kj[t], fv + h)),
            pl.BlockSpec((None, tq, w), lambda b_, h, t, qi, kj, kd: (b_, kj[t], h)),
        ],
        out_specs=pl.BlockSpec((None, tq, w), lambda b_, h, t, qi, kj, kd: (b_, qi[t], h)),
        scratch_shapes=[pltpu.VMEM((heads, 1, tq), F32),
                        pltpu.VMEM((heads, FOX_HD + 16, tq), F32),
                        pltpu.VMEM((heads, tq, 2 * FOX_HD), BF16)],
    )
    return pl.pallas_call(
        functools.partial(_fox_kernel, heads=heads),
        grid_spec=grid_spec,
        out_shape=jax.ShapeDtypeStruct((b, s, FOX_W), BF16),
        compiler_params=_params(("parallel", "parallel", "arbitrary")),
        name="fox_attention",
    )(qi, kj, kind, z3, cqx, zm, zm, ckxm, z3, z3, ckx)


def _ret_kernel(q_ref, k_ref, v_ref, g_ref, km_ref, vm_ref, gn_ref, o_ref, state_sc, decay_sc,
                xi_sc, zeta_sc, cd_sc, *, heads):
    hg = pl.program_id(1)
    c = pl.program_id(2)
    chunk = q_ref.shape[0]
    dk, dv = RET_DK, RET_DV

    def log_gamma(head, width):
        hv = jnp.full((1, width), head, jnp.int32).astype(F32)
        return jnp.log1p(-jnp.exp2(-5.0 - hv))

    def rows(n, width):
        return lax.broadcasted_iota(jnp.int32, (n, width), 0).astype(F32)

    @pl.when(c == 0)
    def _():
        for g in range(heads):
            head = hg * heads + g
            diff = rows(chunk, chunk) - lax.broadcasted_iota(jnp.int32, (chunk, chunk), 1).astype(F32)
            decay_sc[g] = jnp.where(diff >= 0, jnp.exp(log_gamma(head, chunk) * jnp.maximum(diff, 0.0)), 0.0)
            xi_sc[g] = jnp.exp(log_gamma(head, dv) * (rows(chunk, dv) + 1.0))
            zeta_sc[g] = jnp.exp(log_gamma(head, dk) * (chunk - 1.0 - rows(chunk, dk)))
            cd_sc[g] = jnp.exp(log_gamma(head, dv) * float(chunk))
            zeta_m = jnp.exp(log_gamma(head, dk) * (BLOCK - 1.0 - rows(BLOCK, dk)))
            kz = (km_ref[:, g * dk:(g + 1) * dk].astype(F32) * zeta_m).astype(BF16)
            state_sc[g] = _tn(kz, vm_ref[:, g * dv:(g + 1) * dv])

    for g in range(heads):
        q = q_ref[:, g * dk:(g + 1) * dk]
        k = k_ref[:, g * dk:(g + 1) * dk]
        v = v_ref[:, g * dv:(g + 1) * dv]
        s = _nt(q, k) * decay_sc[g]
        intra = jnp.dot(s.astype(BF16), v, preferred_element_type=F32)
        state = state_sc[g]
        cross = jnp.dot(q, state.astype(BF16), preferred_element_type=F32) * xi_sc[g]
        o = intra + cross
        kz = (k.astype(F32) * zeta_sc[g]).astype(BF16)
        state_sc[g] = state * cd_sc[g] + _tn(kz, v)
        mu = jnp.mean(o, axis=-1, keepdims=True)
        d = o - mu
        var = jnp.mean(d * d, axis=-1, keepdims=True)
        y = d * lax.rsqrt(var + EPS) * gn_ref[:, g * dv:(g + 1) * dv]
        o_ref[:, g * dv:(g + 1) * dv] = (g_ref[:, g * dv:(g + 1) * dv].astype(F32) * y).astype(o_ref.dtype)


def _retention(z3, zm, ret_norm_g, *, chunk, heads):
    b, s, _ = z3.shape
    nc = s // chunk
    wk, wv = heads * RET_DK, heads * RET_DV
    rq, rk = Z_RQ // wk, Z_RK // wk
    rv, gr = Z_RV // wv, Z_GR // wv
    return pl.pallas_call(
        functools.partial(_ret_kernel, heads=heads),
        grid=(b, RET_HEADS // heads, nc),
        in_specs=[
            pl.BlockSpec((None, chunk, wk), lambda b_, h, c: (b_, c, rq + h)),
            pl.BlockSpec((None, chunk, wk), lambda b_, h, c: (b_, c, rk + h)),
            pl.BlockSpec((None, chunk, wv), lambda b_, h, c: (b_, c, rv + h)),
            pl.BlockSpec((None, chunk, wv), lambda b_, h, c: (b_, c, gr + h)),
            pl.BlockSpec((BLOCK, wk), lambda b_, h, c: (0, rk + h)),
            pl.BlockSpec((BLOCK, wv), lambda b_, h, c: (0, rv + h)),
            pl.BlockSpec((1, wv), lambda b_, h, c: (0, h)),
        ],
        out_specs=pl.BlockSpec((None, chunk, wv), lambda b_, h, c: (b_, c, h)),
        out_shape=jax.ShapeDtypeStruct((b, s, RET_V_W), BF16),
        scratch_shapes=[pltpu.VMEM((heads, RET_DK, RET_DV), F32),
                        pltpu.VMEM((heads, chunk, chunk), F32),
                        pltpu.VMEM((heads, chunk, RET_DV), F32),
                        pltpu.VMEM((heads, chunk, RET_DK), F32),
                        pltpu.VMEM((heads, 1, RET_DV), F32)],
        compiler_params=_params(("parallel", "parallel", "arbitrary")),
        name="retention",
    )(z3, z3, z3, z3, zm, zm, ret_norm_g.reshape(1, RET_V_W))


def _merge_kernel(ya_ref, yr_ref, wf_ref, wr_ref, ga_ref, gr_ref, o_ref):
    a = jnp.dot(ya_ref[...], wf_ref[...], preferred_element_type=F32)
    r = jnp.dot(yr_ref[...], wr_ref[...], preferred_element_type=F32)
    o_ref[...] = (ga_ref[...].astype(F32) * a + gr_ref[...].astype(F32) * r).astype(o_ref.dtype)


def _merge(ya, yr, wf, wr, z, *, tm, tn, d_model):
    m = ya.shape[0]
    ga0 = Z_GA // tn
    gr0 = (Z_GA + d_model) // tn
    return pl.pallas_call(
        _merge_kernel,
        grid=(m // tm, d_model // tn),
        in_specs=[pl.BlockSpec((tm, FOX_W), lambda i, j: (i, 0)),
                  pl.BlockSpec((tm, RET_V_W), lambda i, j: (i, 0)),
                  pl.BlockSpec((FOX_W, tn), lambda i, j: (0, j)),
                  pl.BlockSpec((RET_V_W, tn), lambda i, j: (0, j)),
                  pl.BlockSpec((tm, tn), lambda i, j: (i, ga0 + j)),
                  pl.BlockSpec((tm, tn), lambda i, j: (i, gr0 + j))],
        out_specs=pl.BlockSpec((tm, tn), lambda i, j: (i, j)),
        out_shape=jax.ShapeDtypeStruct((m, d_model), BF16),
        compiler_params=_params(("parallel", "arbitrary"), V7X_VMEM_BUDGET),
        name="merge_proj",
    )(ya, yr, wf, wr, z, z)


def _outproj_kernel(a_ref, w_ref, x_ref, o_ref):
    o_ref[...] = x_ref[...] + jnp.dot(a_ref[...], w_ref[...], preferred_element_type=F32)


def _outproj(a, w, x, *, tm, tn):
    m, k = a.shape
    n = w.shape[1]
    return pl.pallas_call(
        _outproj_kernel,
        grid=(m // tm, n // tn),
        in_specs=[pl.BlockSpec((tm, k), lambda i, j: (i, 0)),
                  pl.BlockSpec((k, tn), lambda i, j: (0, j)),
                  pl.BlockSpec((tm, tn), lambda i, j: (i, j))],
        out_specs=pl.BlockSpec((tm, tn), lambda i, j: (i, j)),
        out_shape=jax.ShapeDtypeStruct((m, n), F32),
        compiler_params=_params(("parallel", "arbitrary"), V7X_VMEM_BUDGET),
        name="out_proj",
    )(a, w, x)


def _peer_score_kernel(x_ref, w_ref, keys_ref, o_ref):
    acc = jnp.dot(x_ref[...], w_ref[...], preferred_element_type=F32)
    for g in range(acc.shape[1] // PEER_HALF):
        qg = acc[:, g * PEER_HALF:(g + 1) * PEER_HALF].astype(BF16)
        o_ref[g] = _nt(keys_ref[g], qg)


def _peer_scores(xn, wq, keys, *, tm, tn):
    m, d = xn.shape
    n = wq.shape[1]
    ng = tn // PEER_HALF
    return pl.pallas_call(
        _peer_score_kernel,
        grid=(m // tm, n // tn),
        in_specs=[pl.BlockSpec((tm, d), lambda i, j: (i, 0)),
                  pl.BlockSpec((d, tn), lambda i, j: (0, j)),
                  pl.BlockSpec((ng, PEER_NKEYS, PEER_HALF), lambda i, j: (j, 0, 0))],
        out_specs=pl.BlockSpec((ng, PEER_NKEYS, tm), lambda i, j: (j, 0, i)),
        out_shape=jax.ShapeDtypeStruct((n // PEER_HALF, PEER_NKEYS, m), F32),
        compiler_params=_params(("parallel", "arbitrary"), V7X_VMEM_BUDGET),
        name="peer_scores",
    )(xn, wq, keys)


def _top_values(x, n, stacked_rows):
    rows, width = x.shape
    idx = lax.broadcasted_iota(jnp.int32, (rows, width), 0).astype(F32)
    srow = lax.broadcasted_iota(jnp.int32, (stacked_rows, width), 0)
    stacked = jnp.zeros((stacked_rows, width), F32)
    rank = jnp.full((rows, width), float(n), F32)
    vals = []
    for t in range(n):
        cur = jnp.max(x, axis=0, keepdims=True)
        vals.append(cur)
        stacked = jnp.where(srow == t, cur, stacked)
        first = jnp.min(jnp.where(x == cur, idx, float(rows)), axis=0, keepdims=True)
        hit = idx == first
        rank = jnp.where(hit, float(t), rank)
        x = jnp.where(hit, -jnp.inf, x)
    return stacked, vals, rank


def _peer_topk_kernel(s_ref, e1_ref, e2_ref, rank_ref, cnt_ref):
    s1 = s_ref[0]
    s2 = s_ref[1]
    v1, v1l, rank1 = _top_values(s1, PEER_TOPK, PEER_TOPK)
    v2, v2l, _ = _top_values(s2, PEER_TOPK, PEER_TOPK)
    sub = lax.broadcasted_iota(jnp.int32, (8, s1.shape[1]), 0)
    pieces = [v1l[0] + v2]
    for a in range(1, 8):
        nb = PEER_TOPK // (a + 1)
        pieces.append(jnp.where(sub < nb, v1l[a] + v2[0:8], -jnp.inf))
    pieces.append(v1[8:16] + v2l[0])
    cand = jnp.concatenate(pieces, axis=0)
    _, top, _ = _top_values(cand, PEER_TOPK, 8)
    m = top[0]
    z = jnp.zeros_like(m)
    for t in range(PEER_TOPK):
        z = z + jnp.exp(top[t] - m)
    tau = top[PEER_TOPK - 1]
    cnt = jnp.zeros_like(s2)
    for a in range(PEER_TOPK):
        cnt = cnt + jnp.where(v1l[a] + s2 >= tau, 1.0, 0.0)
    rank_ref[...] = rank1
    cnt_ref[...] = cnt.astype(cnt_ref.dtype)
    e1_ref[...] = jnp.exp(s1 - v1l[0])
    e2_ref[...] = (jnp.exp(s2 - v2l[0]) / z).astype(e2_ref.dtype)


def _peer_topk(st, *, tt):
    _, _, t = st.shape
    spec = pl.BlockSpec((None, PEER_NKEYS, tt), lambda i, h: (h, 0, i))
    shape = lambda dt: jax.ShapeDtypeStruct((PEER_HEADS, PEER_NKEYS, t), dt)
    return pl.pallas_call(
        _peer_topk_kernel,
        grid=(t // tt, PEER_HEADS),
        in_specs=[pl.BlockSpec((2, PEER_NKEYS, tt), lambda i, h: (h, 0, i))],
        out_specs=[spec, spec, spec, spec],
        out_shape=[shape(F32), shape(BF16), shape(F32), shape(BF16)],
        compiler_params=_params(("parallel", "parallel")),
        name="peer_topk",
    )(st)


def _gelu_exact(x):
    return 0.5 * x * (1.0 + lax.erf(x * (2.0 ** -0.5)))


def _peer_dense_kernel(xt_ref, u_ref, v_ref, e1_ref, rank_ref, e2_ref, cnt_ref, h_ref, o_ref, gate_sc, *,
                       na, halves):
    j = pl.program_id(1)

    def build_gate(chunk, gate_sc):
        sub = 16
        bshape = (sub, V7X_LANES)
        for a in range(na):
            row = chunk * na + a
            rkrows = [rank_ref[h, pl.ds(row, 1), :] for h in range(PEER_HEADS)]
            e1rows = [e1_ref[h, pl.ds(row, 1), :] for h in range(PEER_HEADS)]
            for l0 in range(0, gate_sc.shape[1], V7X_LANES):
                ls = slice(l0, l0 + V7X_LANES)
                rkb = [jnp.broadcast_to(rkrows[h][:, ls], bshape).astype(BF16) for h in range(PEER_HEADS)]
                e1b = [jnp.broadcast_to(e1rows[h][:, ls], bshape).astype(BF16) for h in range(PEER_HEADS)]
                for r in range(0, PEER_NKEYS, sub):
                    gate = None
                    for h in range(PEER_HEADS):
                        sel = rkb[h] < cnt_ref[h, r:r + sub, ls]
                        term = jnp.where(sel, e1b[h] * e2_ref[h, r:r + sub, ls], jnp.zeros(bshape, BF16))
                        gate = term if gate is None else gate + term
                    gate_sc[a * PEER_NKEYS + r:a * PEER_NKEYS + r + sub, ls] = gate

    @pl.when(j == 0)
    def _():
        o_ref[...] = h_ref[...]

    build_gate(j, gate_sc)
    rows = na * PEER_NKEYS // halves
    total = None
    for k in range(halves):
        sl = slice(k * rows, (k + 1) * rows)
        act = jnp.dot(u_ref[sl, :], xt_ref[...], preferred_element_type=F32)
        w = gate_sc[sl, :] * _gelu_exact(act).astype(BF16)
        part = _tn(w, v_ref[sl, :])
        total = part if total is None else total + part
    o_ref[...] += total


def _peer_dense(xnt, u_tab, v_tab, e1, rank1, e2, cnt, h, *, tt, ec):
    d, t = xnt.shape
    n_exp = u_tab.shape[0]
    na = ec // PEER_NKEYS
    once = pl.Buffered(1)
    kern = functools.partial(_peer_dense_kernel, na=na, halves=2)
    table = pl.BlockSpec((PEER_HEADS, PEER_NKEYS, tt), lambda i, j: (0, 0, i), pipeline_mode=once)
    return pl.pallas_call(
        kern,
        grid=(t // tt, n_exp // ec),
        in_specs=[pl.BlockSpec((d, tt), lambda i, j: (0, i), pipeline_mode=once),
                  pl.BlockSpec((ec, d), lambda i, j: (j, 0)),
                  pl.BlockSpec((ec, d), lambda i, j: (j, 0)),
                  table, table, table, table,
                  pl.BlockSpec((tt, d), lambda i, j: (i, 0), pipeline_mode=once)],
        out_specs=pl.BlockSpec((tt, d), lambda i, j: (i, 0)),
        out_shape=jax.ShapeDtypeStruct((t, d), F32),
        scratch_shapes=[pltpu.VMEM((ec, tt), BF16)],
        compiler_params=_params(("parallel", "arbitrary"), V7X_VMEM_BYTES - 4 * 1024 * 1024),
        name="peer_dense",
    )(xnt, u_tab, v_tab, e1, rank1, e2, cnt, h)


def _rotary_tables(pos, valid):
    half = RET_DK // 2
    inv = ROPE_BASE ** (-jnp.arange(half, dtype=F32) / half)
    ang = pos[:, None] * inv[None, :]
    cos, sin = jnp.cos(ang), jnp.sin(ang)
    c = jnp.concatenate([cos, cos], axis=-1)
    s = jnp.concatenate([-sin, sin], axis=-1)
    kscale = (RET_DK ** -0.5) * valid.astype(F32)[:, None]
    return jnp.stack([c, c * kscale]), jnp.stack([s, s * kscale])


def _tile(n, pref):
    t = min(n, pref)
    assert n % t == 0
    return t


def kernel(x, meta_tokens, norm_mix_g, w_in, b_forget, q_norm_g, k_norm_g, ret_norm_g, w_proj_fox,
           w_proj_ret, w_out, norm_ffn_g, peer_w_q, peer_keys_1, peer_keys_2, peer_u, peer_v):
    b, s, d = x.shape
    depth = w_in.shape[0]
    assert depth == 1, "only the real-token outputs of a single layer are computed"
    m = b * s
    xr = x.reshape(m, d)
    hm = jnp.concatenate([jnp.zeros((PAD, d), x.dtype), meta_tokens.astype(x.dtype)], axis=0)

    w = lax.optimization_barrier(w_in[0].astype(BF16))
    o_fq, o_fk, o_fv, o_f = 0, FOX_W, 2 * FOX_W, 3 * FOX_W
    o_rq = o_f + FOX_HEADS
    o_rk, o_rv = o_rq + RET_QK_W, o_rq + 2 * RET_QK_W
    o_gr = o_rv + RET_V_W
    o_ga = o_gr + RET_V_W
    o_gb = o_ga + d
    col = lambda o, n: w[:, o:o + n]
    wp = jnp.concatenate([col(o_fq, FOX_W), col(o_fk, FOX_W), col(o_rq, RET_QK_W), col(o_rk, RET_QK_W),
                          col(o_fv, FOX_W), col(o_rv, RET_V_W), col(o_gr, RET_V_W), col(o_ga, d),
                          col(o_gb, d)], axis=1).astype(BF16)
    npk = wp.shape[1]
    wf = jnp.pad(col(o_f, FOX_HEADS), ((0, 0), (0, V7X_LANES - FOX_HEADS))).astype(BF16)
    bf = jnp.pad(b_forget[0], (0, V7X_LANES - FOX_HEADS)).reshape(1, V7X_LANES)
    gain = jnp.concatenate([jnp.tile(q_norm_g[0] * (FOX_HD ** -0.5 * LOG2_E), FOX_HEADS),
                            jnp.tile(k_norm_g[0], FOX_HEADS),
                            jnp.zeros((npk - 2 * FOX_W,), F32)]).reshape(1, npk)

    pos_real = jnp.arange(s, dtype=F32) + N_META
    rc, rs = _rotary_tables(pos_real, jnp.ones((s,), bool))
    pos_meta = jnp.arange(BLOCK, dtype=F32) - PAD
    rcm, rsm = _rotary_tables(pos_meta, jnp.arange(BLOCK) >= PAD)

    tm = _tile(m, 1024)
    tn = _tile(d, 1024)

    u = _rmsnorm_bf16(xr, norm_mix_g[0], _tile(m, 512))
    um = _rmsnorm_bf16(hm, norm_mix_g[0], BLOCK)
    z = _proj(u, wp, gain, rc, rs, tm=_tile(s, 1024), tn=tn, d_model=d, first_valid_row=0)
    zm = _proj(um, wp, gain, rcm, rsm, tm=BLOCK, tn=tn, d_model=d, first_valid_row=PAD)
    cm, _, ckxm = _forget_cumsum(um, wf, bf, jnp.zeros((1, V7X_LANES), F32), nb=1, tm=BLOCK,
                                 first_valid_row=PAD)
    _, cqx, ckx = _forget_cumsum(u, wf, bf, cm[BLOCK - 1:BLOCK], nb=b, tm=_tile(s, 512), first_valid_row=0)

    z3 = z.reshape(b, s, npk)
    ya = _fox_attention(z3, zm, cqx.reshape(b, s, V7X_LANES), ckx.reshape(b, s, FOX_W), ckxm,
                        tq=_tile(s, 512), heads=16)
    chunk = _tile(s, 256)
    yr = _retention(z3, zm, ret_norm_g[0], chunk=chunk, heads=8)

    merged = _merge(ya.reshape(m, FOX_W), yr.reshape(m, RET_V_W), w_proj_fox[0].astype(BF16),
                    w_proj_ret[0].astype(BF16), z, tm=tm, tn=_tile(d, 512), d_model=d)
    h1 = _outproj(merged, w_out[0].astype(BF16), xr, tm=tm, tn=tn)

    xn, xnt = _rmsnorm_bf16(h1, norm_ffn_g[0], _tile(m, 512), with_transpose=True)
    keys = jnp.stack([peer_keys_1[0], peer_keys_2[0]], axis=1).reshape(
        2 * PEER_HEADS, PEER_NKEYS, PEER_HALF).astype(BF16)
    st = _peer_scores(xn, peer_w_q[0].astype(BF16), keys, tm=tm, tn=_tile(PEER_HEADS * PEER_QDIM, 1024))
    e1, e2, rank1, cnt = _peer_topk(st, tt=_tile(m, 512))
    out = _peer_dense(xnt, peer_u[0].astype(BF16), peer_v[0].astype(BF16), e1, rank1, e2, cnt, h1,
                      tt=_tile(m, 512), ec=512)
    return out.reshape(b, s, d)
```

```python
import functools

import jax
import jax.numpy as jnp
from jax import lax
from jax.experimental import pallas as pl
from jax.experimental.pallas import tpu as pltpu

N_META = 16
BLOCK = 128
PAD = BLOCK - N_META
EPS = 1e-6
NEG = -1e30
LOG2_E = 1.4426950408889634
FOX_HEADS = 16
FOX_HD = 128
FOX_W = FOX_HEADS * FOX_HD
RET_HEADS = 16
RET_DK = 128
RET_DV = 256
RET_QK_W = RET_HEADS * RET_DK
RET_V_W = RET_HEADS * RET_DV
ROPE_BASE = 10000.0
PEER_HEADS = 8
PEER_NKEYS = 128
PEER_N = PEER_NKEYS * PEER_NKEYS
PEER_QDIM = 256
PEER_HALF = PEER_QDIM // 2
PEER_TOPK = 16

V7X_LANES = 128
V7X_VMEM_BYTES = 64 * 1024 * 1024
V7X_VMEM_BUDGET = 56 * 1024 * 1024

BF16 = jnp.bfloat16
F32 = jnp.float32

Z_FQ = 0
Z_FK = Z_FQ + FOX_W
Z_RQ = Z_FK + FOX_W
Z_RK = Z_RQ + RET_QK_W
Z_FV = Z_RK + RET_QK_W
Z_RV = Z_FV + FOX_W
Z_GR = Z_RV + RET_V_W
Z_GA = Z_GR + RET_V_W


def _params(sem, vmem=None):
    return pltpu.CompilerParams(dimension_semantics=sem, vmem_limit_bytes=vmem)


def _nt(a, b):
    return lax.dot_general(a, b, (((1,), (1,)), ((), ())), preferred_element_type=F32)


def _tn(a, b):
    return lax.dot_general(a, b, (((0,), (0,)), ((), ())), preferred_element_type=F32)


def _sigmoid(x):
    return 0.5 + 0.5 * jnp.tanh(0.5 * x)


def _rmsnorm_kernel(x_ref, g_ref, o_ref, *ot_ref):
    x = x_ref[...]
    ms = jnp.mean(x * x, axis=-1, keepdims=True)
    y = (x * lax.rsqrt(ms + EPS) * g_ref[...]).astype(o_ref.dtype)
    o_ref[...] = y
    if ot_ref:
        ot_ref[0][...] = y.T


def _rmsnorm_bf16(x, g, tm, with_transpose=False):
    m, d = x.shape
    out_specs = [pl.BlockSpec((tm, d), lambda i: (i, 0))]
    out_shape = [jax.ShapeDtypeStruct((m, d), BF16)]
    if with_transpose:
        out_specs.append(pl.BlockSpec((d, tm), lambda i: (0, i)))
        out_shape.append(jax.ShapeDtypeStruct((d, m), BF16))
    out = pl.pallas_call(
        _rmsnorm_kernel,
        grid=(m // tm,),
        in_specs=[pl.BlockSpec((tm, d), lambda i: (i, 0)),
                  pl.BlockSpec((1, d), lambda i: (0, 0))],
        out_specs=out_specs,
        out_shape=out_shape,
        compiler_params=_params(("parallel",), V7X_VMEM_BUDGET),
        name="rmsnorm_bf16",
    )(x, g.reshape(1, d))
    return out if with_transpose else out[0]


def _proj_kernel(u_ref, w_ref, gain_ref, rc_ref, rs_ref, o_ref, *, classes, first_valid_row):
    j = pl.program_id(1)
    acc = jnp.dot(u_ref[...], w_ref[...], preferred_element_type=F32)
    tm, tn = acc.shape
    groups = [slice(g * V7X_LANES, (g + 1) * V7X_LANES) for g in range(tn // V7X_LANES)]

    def emit(kind):
        if kind == "headnorm":
            for sl in groups:
                x = acc[:, sl]
                ms = jnp.mean(x * x, axis=-1, keepdims=True)
                o_ref[:, sl] = (x * lax.rsqrt(ms + EPS) * gain_ref[:, sl]).astype(o_ref.dtype)
        elif kind == "rotary":
            c = rc_ref[...]
            s = rs_ref[...]
            for sl in groups:
                x = acc[:, sl]
                o_ref[:, sl] = (x * c + pltpu.roll(x, V7X_LANES // 2, 1) * s).astype(o_ref.dtype)
        elif kind == "plain":
            o_ref[...] = acc.astype(o_ref.dtype)
        elif kind == "vmasked":
            if first_valid_row > 0:
                rows = lax.broadcasted_iota(jnp.int32, (tm, 1), 0) + pl.program_id(0) * tm
                o_ref[...] = jnp.where(rows >= first_valid_row, acc, 0.0).astype(o_ref.dtype)
            else:
                o_ref[...] = acc.astype(o_ref.dtype)
        elif kind == "silu":
            o_ref[...] = (acc * _sigmoid(acc)).astype(o_ref.dtype)
        elif kind == "sigmoid":
            o_ref[...] = _sigmoid(acc).astype(o_ref.dtype)
        else:
            raise ValueError(kind)

    for kind, lo, hi in classes:
        pl.when((j >= lo) & (j < hi))(functools.partial(emit, kind))


def _proj(u, wp, gain, rot_c, rot_s, *, tm, tn, d_model, first_valid_row):
    m, d = u.shape
    npk = wp.shape[1]
    n_row_tiles = rot_c.shape[1] // tm
    widths = (("headnorm", 2 * FOX_W), ("rotary", 2 * RET_QK_W), ("plain", FOX_W),
              ("vmasked", RET_V_W), ("silu", RET_V_W), ("sigmoid", 2 * d_model))
    classes, lo = [], 0
    for kind, w in widths:
        classes.append((kind, lo, lo + w // tn))
        lo += w // tn
    assert lo * tn == npk
    rk_lo = Z_RK // tn
    kern = functools.partial(_proj_kernel, classes=tuple(classes), first_valid_row=first_valid_row)
    rot_map = lambda i, j: ((j >= rk_lo).astype(jnp.int32), i % n_row_tiles, 0)
    return pl.pallas_call(
        kern,
        grid=(m // tm, npk // tn),
        in_specs=[pl.BlockSpec((tm, d), lambda i, j: (i, 0)),
                  pl.BlockSpec((d, tn), lambda i, j: (0, j)),
                  pl.BlockSpec((1, tn), lambda i, j: (0, j)),
                  pl.BlockSpec((None, tm, V7X_LANES), rot_map),
                  pl.BlockSpec((None, tm, V7X_LANES), rot_map)],
        out_specs=pl.BlockSpec((tm, tn), lambda i, j: (i, j)),
        out_shape=jax.ShapeDtypeStruct((m, npk), BF16),
        compiler_params=_params(("parallel", "arbitrary"), V7X_VMEM_BUDGET),
        name="input_proj",
    )(u, wp, gain, rot_c, rot_s)


def _split3(x):
    hi = x.astype(BF16)
    r = x - hi.astype(F32)
    mid = r.astype(BF16)
    lo = (r - mid.astype(F32)).astype(BF16)
    return hi, mid, lo


def _forget_kernel(u_ref, w_ref, b_ref, cin_ref, selq_ref, selk_ref, o_ref, cqx_ref, ckx_ref, carry_ref, *,
                   first_valid_row):
    i = pl.program_id(1)

    @pl.when(i == 0)
    def _():
        carry_ref[...] = cin_ref[...]

    x = jnp.dot(u_ref[...], w_ref[...], preferred_element_type=F32) + b_ref[...]
    tm = x.shape[0]
    logf = jnp.minimum(x, 0.0) - jnp.log1p(jnp.exp(-jnp.abs(x)))
    rows = lax.broadcasted_iota(jnp.int32, (tm, tm), 0)
    cols = lax.broadcasted_iota(jnp.int32, (tm, tm), 1)
    if first_valid_row > 0:
        r1 = lax.broadcasted_iota(jnp.int32, (tm, 1), 0) + i * tm
        logf = jnp.where(r1 >= first_valid_row, logf, 0.0)
    tri = jnp.where(rows >= cols, 1.0, 0.0).astype(BF16)
    hi, mid, lo = _split3(logf)
    cs = (jnp.dot(tri, hi, preferred_element_type=F32)
          + jnp.dot(tri, mid, preferred_element_type=F32)
          + jnp.dot(tri, lo, preferred_element_type=F32))
    c = cs + carry_ref[...]
    o_ref[...] = c
    carry_ref[...] = c[tm - 1:tm, :]
    chi, cmid, clo = _split3(c * LOG2_E)
    lane = lax.broadcasted_iota(jnp.int32, c.shape, 1)
    one = jnp.where(lane < FOX_HEADS, 1.0, 0.0).astype(BF16)
    pq = jnp.concatenate([chi, cmid, clo, one, one, one], axis=1)
    pk = jnp.concatenate([one, one, one, -chi, -cmid, -clo], axis=1)
    cqx_ref[...] = jnp.dot(pq, selq_ref[...], preferred_element_type=F32).astype(BF16)
    ckx_ref[...] = jnp.dot(pk, selk_ref[...], preferred_element_type=F32).astype(BF16)


FOX_EXT = 6


def _forget_selectors():
    import numpy as np
    selq = np.zeros((FOX_EXT * V7X_LANES, V7X_LANES), np.float32)
    selk = np.zeros((FOX_EXT * V7X_LANES, FOX_W), np.float32)
    for p in range(FOX_EXT):
        for h in range(FOX_HEADS):
            selq[p * V7X_LANES + h, FOX_EXT * h + p] = 1.0
            selk[p * V7X_LANES + h, FOX_HD * h + FOX_EXT * h + p] = 1.0
    return jnp.asarray(selq, BF16), jnp.asarray(selk, BF16)


def _forget_cumsum(u, wf, bf, carry_in, *, nb, tm, first_valid_row):
    m, d = u.shape
    nt = m // (nb * tm)
    selq, selk = _forget_selectors()
    kern = functools.partial(_forget_kernel, first_valid_row=first_valid_row)
    const = lambda b, i: (0, 0)
    row = lambda b, i: (b * nt + i, 0)
    return pl.pallas_call(
        kern,
        grid=(nb, nt),
        in_specs=[pl.BlockSpec((tm, d), row),
                  pl.BlockSpec((d, V7X_LANES), const),
                  pl.BlockSpec((1, V7X_LANES), const),
                  pl.BlockSpec((1, V7X_LANES), const),
                  pl.BlockSpec(selq.shape, const),
                  pl.BlockSpec(selk.shape, const)],
        out_specs=[pl.BlockSpec((tm, V7X_LANES), row),
                   pl.BlockSpec((tm, V7X_LANES), row),
                   pl.BlockSpec((tm, FOX_W), row)],
        out_shape=[jax.ShapeDtypeStruct((m, V7X_LANES), F32),
                   jax.ShapeDtypeStruct((m, V7X_LANES), BF16),
                   jax.ShapeDtypeStruct((m, FOX_W), BF16)],
        scratch_shapes=[pltpu.VMEM((1, V7X_LANES), F32)],
        compiler_params=_params(("parallel", "arbitrary"), V7X_VMEM_BUDGET),
        name="forget_cumsum",
    )(u, wf, bf, carry_in, selq, selk)


def _fox_kernel(qi_ref, kj_ref, kind_ref, q_ref, cqx_ref, km_ref, vm_ref, ckm_ref, k_ref, v_ref,
                ck_ref, o_ref, m_sc, acc_sc, qx_sc, *, heads):
    t = pl.program_id(2)
    kind = kind_ref[t]
    hd = FOX_HD
    lanes = [slice(g * hd, (g + 1) * hd) for g in range(heads)]

    def scores(g, k_r, ck_r):
        sl = lanes[g]
        k_ext = jnp.concatenate([k_r[:, sl], ck_r[:, sl]], axis=1)
        return _nt(k_ext, qx_sc[g])

    def pv(p, v):
        ones = jnp.ones((16, v.shape[0]), v.dtype)
        return jnp.dot(jnp.concatenate([v.T, ones], axis=0), p, preferred_element_type=F32)

    def update(g, s, v):
        m_prev = m_sc[g]
        m_new = jnp.maximum(m_prev, jnp.max(s, axis=0, keepdims=True))
        alpha = jnp.exp2(m_prev - m_new)
        p = jnp.exp2(s - m_new).astype(BF16)
        acc_sc[g] = alpha * acc_sc[g] + pv(p, v)
        m_sc[g] = m_new

    @pl.when(kind == 0)
    def _():
        for g, sl in enumerate(lanes):
            qx_sc[g] = jnp.concatenate([q_ref[:, sl], cqx_ref[...]], axis=1)
            s = scores(g, km_ref, ckm_ref)
            kpos = lax.broadcasted_iota(jnp.int32, s.shape, 0)
            s = jnp.where(kpos >= PAD, s, NEG)
            m = jnp.max(s, axis=0, keepdims=True)
            p = jnp.exp2(s - m).astype(BF16)
            m_sc[g] = m
            acc_sc[g] = pv(p, vm_ref[:, sl])

    @pl.when(kind == 1)
    def _():
        for g, sl in enumerate(lanes):
            update(g, scores(g, k_ref, ck_ref), v_ref[:, sl])

    @pl.when(kind == 2)
    def _():
        for g, sl in enumerate(lanes):
            s = scores(g, k_ref, ck_ref)
            krow = lax.broadcasted_iota(jnp.int32, s.shape, 0)
            qcol = lax.broadcasted_iota(jnp.int32, s.shape, 1)
            s = jnp.where(krow <= qcol, s, NEG)
            update(g, s, v_ref[:, sl])
            acc = acc_sc[g]
            o_ref[:, sl] = (acc[:hd, :] / acc[hd:hd + 1, :]).T.astype(o_ref.dtype)


def _fox_attention(z3, zm, cqx, ckx, ckxm, *, tq, heads):
    b, s, _ = z3.shape
    nq = s // tq
    qi, kj, kind = [], [], []
    for i in range(nq):
        qi += [i] * (i + 2)
        kj += [0] + list(range(i + 1))
        kind += [0] + [1] * i + [2]
    n_steps = len(qi)
    qi = jnp.asarray(qi, jnp.int32)
    kj = jnp.asarray(kj, jnp.int32)
    kind = jnp.asarray(kind, jnp.int32)
    w = heads * FOX_HD
    fq, fk, fv = Z_FQ // w, Z_FK // w, Z_FV // w
    grid_spec = pltpu.PrefetchScalarGridSpec(
        num_scalar_prefetch=3,
        grid=(b, FOX_HEADS // heads, n_steps),
        in_specs=[
            pl.BlockSpec((None, tq, w), lambda b_, h, t, qi, kj, kd: (b_, qi[t], fq + h)),
            pl.BlockSpec((None, tq, V7X_LANES), lambda b_, h, t, qi, kj, kd: (b_, qi[t], 0)),
            pl.BlockSpec((BLOCK, w), lambda b_, h, t, qi, kj, kd: (0, fk + h)),
            pl.BlockSpec((BLOCK, w), lambda b_, h, t, qi, kj, kd: (0, fv + h)),
            pl.BlockSpec((BLOCK, w), lambda b_, h, t, qi, kj, kd: (0, h)),
            pl.BlockSpec((None, tq, w), lambda b_, h, t, qi, kj, kd: (b_, kj[t], fk + h)),
            pl.BlockSpec((None, tq, w), lambda b_, h, t, qi, kj, kd: (b_, kj[t], fv + h)),
            pl.BlockSpec((None, tq, w), lambda b_, h, t, qi, kj, kd: (b_, kj[t], h)),
        ],
        out_specs=pl.BlockSpec((None, tq, w), lambda b_, h, t, qi, kj, kd: (b_, qi[t], h)),
        scratch_shapes=[pltpu.VMEM((heads, 1, tq), F32),
                        pltpu.VMEM((heads, FOX_HD + 16, tq), F32),
                        pltpu.VMEM((heads, tq, 2 * FOX_HD), BF16)],
    )
    return pl.pallas_call(
        functools.partial(_fox_kernel, heads=heads),
        grid_spec=grid_spec,
        out_shape=jax.ShapeDtypeStruct((b, s, FOX_W), BF16),
        compiler_params=_params(("parallel", "parallel", "arbitrary")),
        name="fox_attention",
    )(qi, kj, kind, z3, cqx, zm, zm, ckxm, z3, z3, ckx)


def _ret_kernel(q_ref, k_ref, v_ref, g_ref, km_ref, vm_ref, gn_ref, o_ref, state_sc, decay_sc,
                xi_sc, zeta_sc, cd_sc, *, heads):
    hg = pl.program_id(1)
    c = pl.program_id(2)
    chunk = q_ref.shape[0]
    dk, dv = RET_DK, RET_DV

    def log_gamma(head, width):
        hv = jnp.full((1, width), head, jnp.int32).astype(F32)
        return jnp.log1p(-jnp.exp2(-5.0 - hv))

    def rows(n, width):
        return lax.broadcasted_iota(jnp.int32, (n, width), 0).astype(F32)

    @pl.when(c == 0)
    def _():
        for g in range(heads):
            head = hg * heads + g
            diff = rows(chunk, chunk) - lax.broadcasted_iota(jnp.int32, (chunk, chunk), 1).astype(F32)
            decay_sc[g] = jnp.where(diff >= 0, jnp.exp(log_gamma(head, chunk) * jnp.maximum(diff, 0.0)), 0.0)
            xi_sc[g] = jnp.exp(log_gamma(head, dv) * (rows(chunk, dv) + 1.0))
            zeta_sc[g] = jnp.exp(log_gamma(head, dk) * (chunk - 1.0 - rows(chunk, dk)))
            cd_sc[g] = jnp.exp(log_gamma(head, dv) * float(chunk))
            zeta_m = jnp.exp(log_gamma(head, dk) * (BLOCK - 1.0 - rows(BLOCK, dk)))
            kz = (km_ref[:, g * dk:(g + 1) * dk].astype(F32) * zeta_m).astype(BF16)
            state_sc[g] = _tn(kz, vm_ref[:, g * dv:(g + 1) * dv])

    for g in range(heads):
        q = q_ref[:, g * dk:(g + 1) * dk]
        k = k_ref[:, g * dk:(g + 1) * dk]
        v = v_ref[:, g * dv:(g + 1) * dv]
        s = _nt(q, k) * decay_sc[g]
        intra = jnp.dot(s.astype(BF16), v, preferred_element_type=F32)
        state = state_sc[g]
        cross = jnp.dot(q, state.astype(BF16), preferred_element_type=F32) * xi_sc[g]
        o = intra + cross
        kz = (k.astype(F32) * zeta_sc[g]).astype(BF16)
        state_sc[g] = state * cd_sc[g] + _tn(kz, v)
        mu = jnp.mean(o, axis=-1, keepdims=True)
        d = o - mu
        var = jnp.mean(d * d, axis=-1, keepdims=True)
        y = d * lax.rsqrt(var + EPS) * gn_ref[:, g * dv:(g + 1) * dv]
        o_ref[:, g * dv:(g + 1) * dv] = (g_ref[:, g * dv:(g + 1) * dv].astype(F32) * y).astype(o_ref.dtype)


def _retention(z3, zm, ret_norm_g, *, chunk, heads):
    b, s, _ = z3.shape
    nc = s // chunk
    wk, wv = heads * RET_DK, heads * RET_DV
    rq, rk = Z_RQ // wk, Z_RK // wk
    rv, gr = Z_RV // wv, Z_GR // wv
    return pl.pallas_call(
        functools.partial(_ret_kernel, heads=heads),
        grid=(b, RET_HEADS // heads, nc),
        in_specs=[
            pl.BlockSpec((None, chunk, wk), lambda b_, h, c: (b_, c, rq + h)),
            pl.BlockSpec((None, chunk, wk), lambda b_, h, c: (b_, c, rk + h)),
            pl.BlockSpec((None, chunk, wv), lambda b_, h, c: (b_, c, rv + h)),
            pl.BlockSpec((None, chunk, wv), lambda b_, h, c: (b_, c, gr + h)),
            pl.BlockSpec((BLOCK, wk), lambda b_, h, c: (0, rk + h)),
            pl.BlockSpec((BLOCK, wv), lambda b_, h, c: (0, rv + h)),
            pl.BlockSpec((1, wv), lambda b_, h, c: (0, h)),
        ],
        out_specs=pl.BlockSpec((None, chunk, wv), lambda b_, h, c: (b_, c, h)),
        out_shape=jax.ShapeDtypeStruct((b, s, RET_V_W), BF16),
        scratch_shapes=[pltpu.VMEM((heads, RET_DK, RET_DV), F32),
                        pltpu.VMEM((heads, chunk, chunk), F32),
                        pltpu.VMEM((heads, chunk, RET_DV), F32),
                        pltpu.VMEM((heads, chunk, RET_DK), F32),
                        pltpu.VMEM((heads, 1, RET_DV), F32)],
        compiler_params=_params(("parallel", "parallel", "arbitrary")),
        name="retention",
    )(z3, z3, z3, z3, zm, zm, ret_norm_g.reshape(1, RET_V_W))


def _merge_kernel(ya_ref, yr_ref, wf_ref, wr_ref, ga_ref, gr_ref, o_ref):
    a = jnp.dot(ya_ref[...], wf_ref[...], preferred_element_type=F32)
    r = jnp.dot(yr_ref[...], wr_ref[...], preferred_element_type=F32)
    o_ref[...] = (ga_ref[...].astype(F32) * a + gr_ref[...].astype(F32) * r).astype(o_ref.dtype)


def _merge(ya, yr, wf, wr, z, *, tm, tn, d_model):
    m = ya.shape[0]
    ga0 = Z_GA // tn
    gr0 = (Z_GA + d_model) // tn
    return pl.pallas_call(
        _merge_kernel,
        grid=(m // tm, d_model // tn),
        in_specs=[pl.BlockSpec((tm, FOX_W), lambda i, j: (i, 0)),
                  pl.BlockSpec((tm, RET_V_W), lambda i, j: (i, 0)),
                  pl.BlockSpec((FOX_W, tn), lambda i, j: (0, j)),
                  pl.BlockSpec((RET_V_W, tn), lambda i, j: (0, j)),
                  pl.BlockSpec((tm, tn), lambda i, j: (i, ga0 + j)),
                  pl.BlockSpec((tm, tn), lambda i, j: (i, gr0 + j))],
        out_specs=pl.BlockSpec((tm, tn), lambda i, j: (i, j)),
        out_shape=jax.ShapeDtypeStruct((m, d_model), BF16),
        compiler_params=_params(("parallel", "arbitrary"), V7X_VMEM_BUDGET),
        name="merge_proj",
    )(ya, yr, wf, wr, z, z)


def _outproj_kernel(a_ref, w_ref, x_ref, o_ref):
    o_ref[...] = x_ref[...] + jnp.dot(a_ref[...], w_ref[...], preferred_element_type=F32)


def _outproj(a, w, x, *, tm, tn):
    m, k = a.shape
    n = w.shape[1]
    return pl.pallas_call(
        _outproj_kernel,
        grid=(m // tm, n // tn),
        in_specs=[pl.BlockSpec((tm, k), lambda i, j: (i, 0)),
                  pl.BlockSpec((k, tn), lambda i, j: (0, j)),
                  pl.BlockSpec((tm, tn), lambda i, j: (i, j))],
        out_specs=pl.BlockSpec((tm, tn), lambda i, j: (i, j)),
        out_shape=jax.ShapeDtypeStruct((m, n), F32),
        compiler_params=_params(("parallel", "arbitrary"), V7X_VMEM_BUDGET),
        name="out_proj",
    )(a, w, x)


def _peer_score_kernel(x_ref, w_ref, keys_ref, o_ref):
    acc = jnp.dot(x_ref[...], w_ref[...], preferred_element_type=F32)
    for g in range(acc.shape[1] // PEER_HALF):
        qg = acc[:, g * PEER_HALF:(g + 1) * PEER_HALF].astype(BF16)
        o_ref[g] = _nt(keys_ref[g], qg)


def _peer_scores(xn, wq, keys, *, tm, tn):
    m, d = xn.shape
    n = wq.shape[1]
    ng = tn // PEER_HALF
    return pl.pallas_call(
        _peer_score_kernel,
        grid=(m // tm, n // tn),
        in_specs=[pl.BlockSpec((tm, d), lambda i, j: (i, 0)),
                  pl.BlockSpec((d, tn), lambda i, j: (0, j)),
                  pl.BlockSpec((ng, PEER_NKEYS, PEER_HALF), lambda i, j: (j, 0, 0))],
        out_specs=pl.BlockSpec((ng, PEER_NKEYS, tm), lambda i, j: (j, 0, i)),
        out_shape=jax.ShapeDtypeStruct((n // PEER_HALF, PEER_NKEYS, m), F32),
        compiler_params=_params(("parallel", "arbitrary"), V7X_VMEM_BUDGET),
        name="peer_scores",
    )(xn, wq, keys)


def _top_values(x, n, stacked_rows):
    rows, width = x.shape
    idx = lax.broadcasted_iota(jnp.int32, (rows, width), 0).astype(F32)
    srow = lax.broadcasted_iota(jnp.int32, (stacked_rows, width), 0)
    stacked = jnp.zeros((stacked_rows, width), F32)
    rank = jnp.full((rows, width), float(n), F32)
    vals = []
    for t in range(n):
        cur = jnp.max(x, axis=0, keepdims=True)
        vals.append(cur)
        stacked = jnp.where(srow == t, cur, stacked)
        first = jnp.min(jnp.where(x == cur, idx, float(rows)), axis=0, keepdims=True)
        hit = idx == first
        rank = jnp.where(hit, float(t), rank)
        x = jnp.where(hit, -jnp.inf, x)
    return stacked, vals, rank


def _peer_topk_kernel(s_ref, e1_ref, e2_ref, rank_ref, cnt_ref):
    s1 = s_ref[0]
    s2 = s_ref[1]
    v1, v1l, rank1 = _top_values(s1, PEER_TOPK, PEER_TOPK)
    v2, v2l, _ = _top_values(s2, PEER_TOPK, PEER_TOPK)
    sub = lax.broadcasted_iota(jnp.int32, (8, s1.shape[1]), 0)
    pieces = [v1l[0] + v2]
    for a in range(1, 8):
        nb = PEER_TOPK // (a + 1)
        pieces.append(jnp.where(sub < nb, v1l[a] + v2[0:8], -jnp.inf))
    pieces.append(v1[8:16] + v2l[0])
    cand = jnp.concatenate(pieces, axis=0)
    _, top, _ = _top_values(cand, PEER_TOPK, 8)
    m = top[0]
    z = jnp.zeros_like(m)
    for t in range(PEER_TOPK):
        z = z + jnp.exp(top[t] - m)
    tau = top[PEER_TOPK - 1]
    cnt = jnp.zeros_like(s2)
    for a in range(PEER_TOPK):
        cnt = cnt + jnp.where(v1l[a] + s2 >= tau, 1.0, 0.0)
    rank_ref[...] = rank1
    cnt_ref[...] = cnt.astype(cnt_ref.dtype)
    e1_ref[...] = jnp.exp(s1 - v1l[0])
    e2_ref[...] = (jnp.exp(s2 - v2l[0]) / z).astype(e2_ref.dtype)


def _peer_topk(st, *, tt):
    _, _, t = st.shape
    spec = pl.BlockSpec((None, PEER_NKEYS, tt), lambda i, h: (h, 0, i))
    shape = lambda dt: jax.ShapeDtypeStruct((PEER_HEADS, PEER_NKEYS, t), dt)
    return pl.pallas_call(
        _peer_topk_kernel,
        grid=(t // tt, PEER_HEADS),
        in_specs=[pl.BlockSpec((2, PEER_NKEYS, tt), lambda i, h: (h, 0, i))],
        out_specs=[spec, spec, spec, spec],
        out_shape=[shape(F32), shape(BF16), shape(F32), shape(BF16)],
        compiler_params=_params(("parallel", "parallel")),
        name="peer_topk",
    )(st)


def _gelu_exact(x):
    return 0.5 * x * (1.0 + lax.erf(x * (2.0 ** -0.5)))


def _peer_dense_kernel(xt_ref, u_ref, v_ref, e1_ref, rank_ref, e2_ref, cnt_ref, h_ref, o_ref, gate_sc, *,
                       na, halves):
    j = pl.program_id(1)

    def build_gate(chunk, gate_sc):
        sub = 16
        bshape = (sub, V7X_LANES)
        for a in range(na):
            row = chunk * na + a
            rkrows = [rank_ref[h, pl.ds(row, 1), :] for h in range(PEER_HEADS)]
            e1rows = [e1_ref[h, pl.ds(row, 1), :] for h in range(PEER_HEADS)]
            for l0 in range(0, gate_sc.shape[1], V7X_LANES):
                ls = slice(l0, l0 + V7X_LANES)
                rkb = [jnp.broadcast_to(rkrows[h][:, ls], bshape).astype(BF16) for h in range(PEER_HEADS)]
                e1b = [jnp.broadcast_to(e1rows[h][:, ls], bshape).astype(BF16) for h in range(PEER_HEADS)]
                for r in range(0, PEER_NKEYS, sub):
                    gate = None
                    for h in range(PEER_HEADS):
                        sel = rkb[h] < cnt_ref[h, r:r + sub, ls]
                        term = jnp.where(sel, e1b[h] * e2_ref[h, r:r + sub, ls], jnp.zeros(bshape, BF16))
                        gate = term if gate is None else gate + term
                    gate_sc[a * PEER_NKEYS + r:a * PEER_NKEYS + r + sub, ls] = gate

    @pl.when(j == 0)
    def _():
        o_ref[...] = h_ref[...]

    build_gate(j, gate_sc)
    rows = na * PEER_NKEYS // halves
    total = None
    for k in range(halves):
        sl = slice(k * rows, (k + 1) * rows)
        act = jnp.dot(u_ref[sl, :], xt_ref[...], preferred_element_type=F32)
        w = gate_sc[sl, :] * _gelu_exact(act).astype(BF16)
        part = _tn(w, v_ref[sl, :])
        total = part if total is None else total + part
    o_ref[...] += total


def _peer_dense(xnt, u_tab, v_tab, e1, rank1, e2, cnt, h, *, tt, ec):
    d, t = xnt.shape
    n_exp = u_tab.shape[0]
    na = ec // PEER_NKEYS
    once = pl.Buffered(1)
    kern = functools.partial(_peer_dense_kernel, na=na, halves=2)
    table = pl.BlockSpec((PEER_HEADS, PEER_NKEYS, tt), lambda i, j: (0, 0, i), pipeline_mode=once)
    return pl.pallas_call(
        kern,
        grid=(t // tt, n_exp // ec),
        in_specs=[pl.BlockSpec((d, tt), lambda i, j: (0, i), pipeline_mode=once),
                  pl.BlockSpec((ec, d), lambda i, j: (j, 0)),
                  pl.BlockSpec((ec, d), lambda i, j: (j, 0)),
                  table, table, table, table,
                  pl.BlockSpec((tt, d), lambda i, j: (i, 0), pipeline_mode=once)],
        out_specs=pl.BlockSpec((tt, d), lambda i, j: (i, 0)),
        out_shape=jax.ShapeDtypeStruct((t, d), F32),
        scratch_shapes=[pltpu.VMEM((ec, tt), BF16)],
        compiler_params=_params(("parallel", "arbitrary"), V7X_VMEM_BYTES - 4 * 1024 * 1024),
        name="peer_dense",
    )(xnt, u_tab, v_tab, e1, rank1, e2, cnt, h)


def _repack_kernel(src_ref, shifted_ref, a_ref, b_ref, o_ref):
    j = pl.program_id(1)
    tn = o_ref.shape[1]

    @pl.when(shifted_ref[j] == 0)
    def _():
        o_ref[...] = a_ref[...].astype(o_ref.dtype)

    @pl.when(shifted_ref[j] != 0)
    def _():
        x = jnp.concatenate([a_ref[...], b_ref[...]], axis=1)
        x = pltpu.roll(x, x.shape[1] - FOX_HEADS, 1)
        o_ref[...] = x[:, :tn].astype(o_ref.dtype)


def _repack_w_in(w, d_model, *, tr, tn):
    k, _ = w.shape
    o_f = 3 * FOX_W
    groups = [(0, FOX_W), (FOX_W, FOX_W),
              (o_f + FOX_HEADS, RET_QK_W), (o_f + FOX_HEADS + RET_QK_W, RET_QK_W),
              (2 * FOX_W, FOX_W),
              (o_f + FOX_HEADS + 2 * RET_QK_W, 2 * RET_V_W + 2 * d_model)]
    src, shifted = [], []
    for start, width in groups:
        off = start % tn
        assert off in (0, FOX_HEADS) and width % tn == 0
        for t in range(width // tn):
            src.append((start - off) // tn + t)
            shifted.append(int(off != 0))
    npk = len(src) * tn
    lanes_per_tile = tn // V7X_LANES
    grid_spec = pltpu.PrefetchScalarGridSpec(
        num_scalar_prefetch=2,
        grid=(k // tr, len(src)),
        in_specs=[pl.BlockSpec((tr, tn), lambda i, j, src, sh: (i, src[j])),
                  pl.BlockSpec((tr, V7X_LANES), lambda i, j, src, sh: (i, (src[j] + 1) * lanes_per_tile))],
        out_specs=pl.BlockSpec((tr, tn), lambda i, j, src, sh: (i, j)),
    )
    return pl.pallas_call(
        _repack_kernel,
        grid_spec=grid_spec,
        out_shape=jax.ShapeDtypeStruct((k, npk), BF16),
        compiler_params=_params(("parallel", "arbitrary")),
        name="repack_w_in",
    )(jnp.asarray(src, jnp.int32), jnp.asarray(shifted, jnp.int32), w, w)


def _rotary_tables(pos, valid):
    half = RET_DK // 2
    inv = ROPE_BASE ** (-jnp.arange(half, dtype=F32) / half)
    ang = pos[:, None] * inv[None, :]
    cos, sin = jnp.cos(ang), jnp.sin(ang)
    c = jnp.concatenate([cos, cos], axis=-1)
    s = jnp.concatenate([-sin, sin], axis=-1)
    kscale = (RET_DK ** -0.5) * valid.astype(F32)[:, None]
    return jnp.stack([c, c * kscale]), jnp.stack([s, s * kscale])


def _tile(n, pref):
    t = min(n, pref)
    assert n % t == 0
    return t


def kernel(x, meta_tokens, norm_mix_g, w_in, b_forget, q_norm_g, k_norm_g, ret_norm_g, w_proj_fox,
           w_proj_ret, w_out, norm_ffn_g, peer_w_q, peer_keys_1, peer_keys_2, peer_u, peer_v):
    b, s, d = x.shape
    depth = w_in.shape[0]
    assert depth == 1, "only the real-token outputs of a single layer are computed"
    m = b * s
    xr = x.reshape(m, d)
    hm = jnp.concatenate([jnp.zeros((PAD, d), x.dtype), meta_tokens.astype(x.dtype)], axis=0)

    tm = _tile(m, 1024)
    tn = _tile(d, 1024)
    wp = _repack_w_in(w_in[0], d, tr=_tile(d, 512), tn=tn)
    npk = wp.shape[1]
    o_f = 3 * FOX_W
    wf = jnp.pad(w_in[0][:, o_f:o_f + FOX_HEADS], ((0, 0), (0, V7X_LANES - FOX_HEADS))).astype(BF16)
    bf = jnp.pad(b_forget[0], (0, V7X_LANES - FOX_HEADS)).reshape(1, V7X_LANES)
    gain = jnp.concatenate([jnp.tile(q_norm_g[0] * (FOX_HD ** -0.5 * LOG2_E), FOX_HEADS),
                            jnp.tile(k_norm_g[0], FOX_HEADS),
                            jnp.zeros((npk - 2 * FOX_W,), F32)]).reshape(1, npk)

    pos_real = jnp.arange(s, dtype=F32) + N_META
    rc, rs = _rotary_tables(pos_real, jnp.ones((s,), bool))
    pos_meta = jnp.arange(BLOCK, dtype=F32) - PAD
    rcm, rsm = _rotary_tables(pos_meta, jnp.arange(BLOCK) >= PAD)

    u = _rmsnorm_bf16(xr, norm_mix_g[0], _tile(m, 512))
    um = _rmsnorm_bf16(hm, norm_mix_g[0], BLOCK)
    z = _proj(u, wp, gain, rc, rs, tm=_tile(s, 1024), tn=tn, d_model=d, first_valid_row=0)
    zm = _proj(um, wp, gain, rcm, rsm, tm=BLOCK, tn=tn, d_model=d, first_valid_row=PAD)
    cm, _, ckxm = _forget_cumsum(um, wf, bf, jnp.zeros((1, V7X_LANES), F32), nb=1, tm=BLOCK,
                                 first_valid_row=PAD)
    _, cqx, ckx = _forget_cumsum(u, wf, bf, cm[BLOCK - 1:BLOCK], nb=b, tm=_tile(s, 512), first_valid_row=0)

    z3 = z.reshape(b, s, npk)
    ya = _fox_attention(z3, zm, cqx.reshape(b, s, V7X_LANES), ckx.reshape(b, s, FOX_W), ckxm,
                        tq=_tile(s, 512), heads=16)
    chunk = _tile(s, 256)
    yr = _retention(z3, zm, ret_norm_g[0], chunk=chunk, heads=8)

    merged = _merge(ya.reshape(m, FOX_W), yr.reshape(m, RET_V_W), w_proj_fox[0].astype(BF16),
                    w_proj_ret[0].astype(BF16), z, tm=tm, tn=_tile(d, 512), d_model=d)
    h1 = _outproj(merged, w_out[0].astype(BF16), xr, tm=tm, tn=tn)

    xn, xnt = _rmsnorm_bf16(h1, norm_ffn_g[0], _tile(m, 512), with_transpose=True)
    keys = jnp.stack([peer_keys_1[0], peer_keys_2[0]], axis=1).reshape(
        2 * PEER_HEADS, PEER_NKEYS, PEER_HALF).astype(BF16)
    st = _peer_scores(xn, peer_w_q[0].astype(BF16), keys, tm=tm, tn=_tile(PEER_HEADS * PEER_QDIM, 1024))
    e1, e2, rank1, cnt = _peer_topk(st, tt=_tile(m, 512))
    out = _peer_dense(xnt, peer_u[0].astype(BF16), peer_v[0].astype(BF16), e1, rank1, e2, cnt, h1,
                      tt=_tile(m, 512), ec=512)
    return out.reshape(b, s, d)
```

```python
import functools

import jax
import jax.numpy as jnp
from jax import lax
from jax.experimental import pallas as pl
from jax.experimental.pallas import tpu as pltpu

N_META = 16
BLOCK = 128
PAD = BLOCK - N_META
EPS = 1e-6
NEG = -1e30
LOG2_E = 1.4426950408889634
FOX_HEADS = 16
FOX_HD = 128
FOX_W = FOX_HEADS * FOX_HD
RET_HEADS = 16
RET_DK = 128
RET_DV = 256
RET_QK_W = RET_HEADS * RET_DK
RET_V_W = RET_HEADS * RET_DV
ROPE_BASE = 10000.0
PEER_HEADS = 8
PEER_NKEYS = 128
PEER_N = PEER_NKEYS * PEER_NKEYS
PEER_QDIM = 256
PEER_HALF = PEER_QDIM // 2
PEER_TOPK = 16

V7X_LANES = 128
V7X_VMEM_BYTES = 64 * 1024 * 1024
V7X_VMEM_BUDGET = 56 * 1024 * 1024

BF16 = jnp.bfloat16
F32 = jnp.float32

Z_FQ = 0
Z_FK = Z_FQ + FOX_W
Z_RQ = Z_FK + FOX_W
Z_RK = Z_RQ + RET_QK_W
Z_FV = Z_RK + RET_QK_W
Z_RV = Z_FV + FOX_W
Z_GR = Z_RV + RET_V_W
Z_GA = Z_GR + RET_V_W


def _params(sem, vmem=None):
    return pltpu.CompilerParams(dimension_semantics=sem, vmem_limit_bytes=vmem)


def _nt(a, b):
    return lax.dot_general(a, b, (((1,), (1,)), ((), ())), preferred_element_type=F32)


def _tn(a, b):
    return lax.dot_general(a, b, (((0,), (0,)), ((), ())), preferred_element_type=F32)


def _sigmoid(x):
    return 0.5 + 0.5 * jnp.tanh(0.5 * x)


def _rmsnorm_kernel(x_ref, g_ref, o_ref, *ot_ref):
    x = x_ref[...]
    ms = jnp.mean(x * x, axis=-1, keepdims=True)
    y = (x * lax.rsqrt(ms + EPS) * g_ref[...]).astype(o_ref.dtype)
    o_ref[...] = y
    if ot_ref:
        ot_ref[0][...] = y.T


def _rmsnorm_bf16(x, g, tm, with_transpose=False):
    m, d = x.shape
    out_specs = [pl.BlockSpec((tm, d), lambda i: (i, 0))]
    out_shape = [jax.ShapeDtypeStruct((m, d), BF16)]
    if with_transpose:
        out_specs.append(pl.BlockSpec((d, tm), lambda i: (0, i)))
        out_shape.append(jax.ShapeDtypeStruct((d, m), BF16))
    out = pl.pallas_call(
        _rmsnorm_kernel,
        grid=(m // tm,),
        in_specs=[pl.BlockSpec((tm, d), lambda i: (i, 0)),
                  pl.BlockSpec((1, d), lambda i: (0, 0))],
        out_specs=out_specs,
        out_shape=out_shape,
        compiler_params=_params(("parallel",), V7X_VMEM_BUDGET),
        name="rmsnorm_bf16",
    )(x, g.reshape(1, d))
    return out if with_transpose else out[0]


def _proj_kernel(u_ref, w_ref, gain_ref, rc_ref, rs_ref, o_ref, *, classes, first_valid_row):
    j = pl.program_id(1)
    acc = _nt(u_ref[...], w_ref[...])
    tm, tn = acc.shape
    groups = [slice(g * V7X_LANES, (g + 1) * V7X_LANES) for g in range(tn // V7X_LANES)]

    def emit(kind):
        if kind == "headnorm":
            for sl in groups:
                x = acc[:, sl]
                ms = jnp.mean(x * x, axis=-1, keepdims=True)
                o_ref[:, sl] = (x * lax.rsqrt(ms + EPS) * gain_ref[:, sl]).astype(o_ref.dtype)
        elif kind == "rotary":
            c = rc_ref[...]
            s = rs_ref[...]
            for sl in groups:
                x = acc[:, sl]
                o_ref[:, sl] = (x * c + pltpu.roll(x, V7X_LANES // 2, 1) * s).astype(o_ref.dtype)
        elif kind == "plain":
            o_ref[...] = acc.astype(o_ref.dtype)
        elif kind == "vmasked":
            if first_valid_row > 0:
                rows = lax.broadcasted_iota(jnp.int32, (tm, 1), 0) + pl.program_id(0) * tm
                o_ref[...] = jnp.where(rows >= first_valid_row, acc, 0.0).astype(o_ref.dtype)
            else:
                o_ref[...] = acc.astype(o_ref.dtype)
        elif kind == "silu":
            o_ref[...] = (acc * _sigmoid(acc)).astype(o_ref.dtype)
        elif kind == "sigmoid":
            o_ref[...] = _sigmoid(acc).astype(o_ref.dtype)
        else:
            raise ValueError(kind)

    for kind, lo, hi in classes:
        pl.when((j >= lo) & (j < hi))(functools.partial(emit, kind))


def _proj(u, wpt, gain, rot_c, rot_s, *, tm, tn, d_model, first_valid_row):
    m, d = u.shape
    npk = wpt.shape[0]
    n_row_tiles = rot_c.shape[1] // tm
    widths = (("headnorm", 2 * FOX_W), ("rotary", 2 * RET_QK_W), ("plain", FOX_W),
              ("vmasked", RET_V_W), ("silu", RET_V_W), ("sigmoid", 2 * d_model))
    classes, lo = [], 0
    for kind, w in widths:
        classes.append((kind, lo, lo + w // tn))
        lo += w // tn
    assert lo * tn == npk
    rk_lo = Z_RK // tn
    kern = functools.partial(_proj_kernel, classes=tuple(classes), first_valid_row=first_valid_row)
    rot_map = lambda i, j: ((j >= rk_lo).astype(jnp.int32), i % n_row_tiles, 0)
    return pl.pallas_call(
        kern,
        grid=(m // tm, npk // tn),
        in_specs=[pl.BlockSpec((tm, d), lambda i, j: (i, 0)),
                  pl.BlockSpec((tn, d), lambda i, j: (j, 0)),
                  pl.BlockSpec((1, tn), lambda i, j: (0, j)),
                  pl.BlockSpec((None, tm, V7X_LANES), rot_map),
                  pl.BlockSpec((None, tm, V7X_LANES), rot_map)],
        out_specs=pl.BlockSpec((tm, tn), lambda i, j: (i, j)),
        out_shape=jax.ShapeDtypeStruct((m, npk), BF16),
        compiler_params=_params(("parallel", "arbitrary"), V7X_VMEM_BUDGET),
        name="input_proj",
    )(u, wpt, gain, rot_c, rot_s)


def _split3(x):
    hi = x.astype(BF16)
    r = x - hi.astype(F32)
    mid = r.astype(BF16)
    lo = (r - mid.astype(F32)).astype(BF16)
    return hi, mid, lo


def _forget_kernel(u_ref, w_ref, b_ref, cin_ref, selq_ref, selk_ref, o_ref, cqx_ref, ckx_ref, carry_ref, *,
                   first_valid_row):
    i = pl.program_id(1)

    @pl.when(i == 0)
    def _():
        carry_ref[...] = cin_ref[...]

    x = _nt(u_ref[...], w_ref[...]) + b_ref[...]
    tm = x.shape[0]
    logf = jnp.minimum(x, 0.0) - jnp.log1p(jnp.exp(-jnp.abs(x)))
    rows = lax.broadcasted_iota(jnp.int32, (tm, tm), 0)
    cols = lax.broadcasted_iota(jnp.int32, (tm, tm), 1)
    if first_valid_row > 0:
        r1 = lax.broadcasted_iota(jnp.int32, (tm, 1), 0) + i * tm
        logf = jnp.where(r1 >= first_valid_row, logf, 0.0)
    tri = jnp.where(rows >= cols, 1.0, 0.0).astype(BF16)
    hi, mid, lo = _split3(logf)
    cs = (jnp.dot(tri, hi, preferred_element_type=F32)
          + jnp.dot(tri, mid, preferred_element_type=F32)
          + jnp.dot(tri, lo, preferred_element_type=F32))
    c = cs + carry_ref[...]
    o_ref[...] = c
    carry_ref[...] = c[tm - 1:tm, :]
    chi, cmid, clo = _split3(c * LOG2_E)
    lane = lax.broadcasted_iota(jnp.int32, c.shape, 1)
    one = jnp.where(lane < FOX_HEADS, 1.0, 0.0).astype(BF16)
    pq = jnp.concatenate([chi, cmid, clo, one, one, one], axis=1)
    pk = jnp.concatenate([one, one, one, -chi, -cmid, -clo], axis=1)
    cqx_ref[...] = jnp.dot(pq, selq_ref[...], preferred_element_type=F32).astype(BF16)
    ckx_ref[...] = jnp.dot(pk, selk_ref[...], preferred_element_type=F32).astype(BF16)


FOX_EXT = 6


def _forget_selectors():
    import numpy as np
    selq = np.zeros((FOX_EXT * V7X_LANES, V7X_LANES), np.float32)
    selk = np.zeros((FOX_EXT * V7X_LANES, FOX_W), np.float32)
    for p in range(FOX_EXT):
        for h in range(FOX_HEADS):
            selq[p * V7X_LANES + h, FOX_EXT * h + p] = 1.0
            selk[p * V7X_LANES + h, FOX_HD * h + FOX_EXT * h + p] = 1.0
    return jnp.asarray(selq, BF16), jnp.asarray(selk, BF16)


def _forget_cumsum(u, wf, bf, carry_in, *, nb, tm, first_valid_row):
    m, d = u.shape
    nt = m // (nb * tm)
    selq, selk = _forget_selectors()
    kern = functools.partial(_forget_kernel, first_valid_row=first_valid_row)
    const = lambda b, i: (0, 0)
    row = lambda b, i: (b * nt + i, 0)
    return pl.pallas_call(
        kern,
        grid=(nb, nt),
        in_specs=[pl.BlockSpec((tm, d), row),
                  pl.BlockSpec((V7X_LANES, d), const),
                  pl.BlockSpec((1, V7X_LANES), const),
                  pl.BlockSpec((1, V7X_LANES), const),
                  pl.BlockSpec(selq.shape, const),
                  pl.BlockSpec(selk.shape, const)],
        out_specs=[pl.BlockSpec((tm, V7X_LANES), row),
                   pl.BlockSpec((tm, V7X_LANES), row),
                   pl.BlockSpec((tm, FOX_W), row)],
        out_shape=[jax.ShapeDtypeStruct((m, V7X_LANES), F32),
                   jax.ShapeDtypeStruct((m, V7X_LANES), BF16),
                   jax.ShapeDtypeStruct((m, FOX_W), BF16)],
        scratch_shapes=[pltpu.VMEM((1, V7X_LANES), F32)],
        compiler_params=_params(("parallel", "arbitrary"), V7X_VMEM_BUDGET),
        name="forget_cumsum",
    )(u, wf, bf, carry_in, selq, selk)


def _fox_kernel(qi_ref, kj_ref, kind_ref, q_ref, cqx_ref, km_ref, vm_ref, ckm_ref, k_ref, v_ref,
                ck_ref, o_ref, m_sc, acc_sc, qx_sc, *, heads):
    t = pl.program_id(2)
    kind = kind_ref[t]
    hd = FOX_HD
    lanes = [slice(g * hd, (g + 1) * hd) for g in range(heads)]

    def scores(g, k_r, ck_r):
        sl = lanes[g]
        k_ext = jnp.concatenate([k_r[:, sl], ck_r[:, sl]], axis=1)
        return _nt(k_ext, qx_sc[g])

    def pv(p, v):
        ones = jnp.ones((16, v.shape[0]), v.dtype)
        return jnp.dot(jnp.concatenate([v.T, ones], axis=0), p, preferred_element_type=F32)

    def update(g, s, v):
        m_prev = m_sc[g]
        m_new = jnp.maximum(m_prev, jnp.max(s, axis=0, keepdims=True))
        alpha = jnp.exp2(m_prev - m_new)
        p = jnp.exp2(s - m_new).astype(BF16)
        acc_sc[g] = alpha * acc_sc[g] + pv(p, v)
        m_sc[g] = m_new

    @pl.when(kind == 0)
    def _():
        for g, sl in enumerate(lanes):
            qx_sc[g] = jnp.concatenate([q_ref[:, sl], cqx_ref[...]], axis=1)
            s = scores(g, km_ref, ckm_ref)
            kpos = lax.broadcasted_iota(jnp.int32, s.shape, 0)
            s = jnp.where(kpos >= PAD, s, NEG)
            m = jnp.max(s, axis=0, keepdims=True)
            p = jnp.exp2(s - m).astype(BF16)
            m_sc[g] = m
            acc_sc[g] = pv(p, vm_ref[:, sl])

    @pl.when(kind == 1)
    def _():
        for g, sl in enumerate(lanes):
            update(g, scores(g, k_ref, ck_ref), v_ref[:, sl])

    @pl.when(kind == 2)
    def _():
        for g, sl in enumerate(lanes):
            s = scores(g, k_ref, ck_ref)
            krow = lax.broadcasted_iota(jnp.int32, s.shape, 0)
            qcol = lax.broadcasted_iota(jnp.int32, s.shape, 1)
            s = jnp.where(krow <= qcol, s, NEG)
            update(g, s, v_ref[:, sl])
            acc = acc_sc[g]
            o_ref[:, sl] = (acc[:hd, :] / acc[hd:hd + 1, :]).T.astype(o_ref.dtype)


def _fox_attention(z3, zm, cqx, ckx, ckxm, *, tq, heads):
    b, s, _ = z3.shape
    nq = s // tq
    qi, kj, kind = [], [], []
    for i in range(nq):
        qi += [i] * (i + 2)
        kj += [0] + list(range(i + 1))
        kind += [0] + [1] * i + [2]
    n_steps = len(qi)
    qi = jnp.asarray(qi, jnp.int32)
    kj = jnp.asarray(kj, jnp.int32)
    kind = jnp.asarray(kind, jnp.int32)
    w = heads * FOX_HD
    fq, fk, fv = Z_FQ // w, Z_FK // w, Z_FV // w
    grid_spec = pltpu.PrefetchScalarGridSpec(
        num_scalar_prefetch=3,
        grid=(b, FOX_HEADS // heads, n_steps),
        in_specs=[
            pl.BlockSpec((None, tq, w), lambda b_, h, t, qi, kj, kd: (b_, qi[t], fq + h)),
            pl.BlockSpec((None, tq, V7X_LANES), lambda b_, h, t, qi, kj, kd: (b_, qi[t], 0)),
            pl.BlockSpec((BLOCK, w), lambda b_, h, t, qi, kj, kd: (0, fk + h)),
            pl.BlockSpec((BLOCK, w), lambda b_, h, t, qi, kj, kd: (0, fv + h)),
            pl.BlockSpec((BLOCK, w), lambda b_, h, t, qi, kj, kd: (0, h)),
            pl.BlockSpec((None, tq, w), lambda b_, h, t, qi, kj, kd: (b_, kj[t], fk + h)),
            pl.BlockSpec((None, tq, w), lambda b_, h, t, qi, kj, kd: (b_, kj[t], fv + h)),
            pl.BlockSpec((None, tq, w), lambda b_, h, t, qi, kj, kd: (b_, kj[t], h)),
        ],
        out_specs=pl.BlockSpec((None, tq, w), lambda b_, h, t, qi, kj, kd: (b_, qi[t], h)),
        scratch_shapes=[pltpu.VMEM((heads, 1, tq), F32),
                        pltpu.VMEM((heads, FOX_HD + 16, tq), F32),
                        pltpu.VMEM((heads, tq, 2 * FOX_HD), BF16)],
    )
    return pl.pallas_call(
        functools.partial(_fox_kernel, heads=heads),
        grid_spec=grid_spec,
        out_shape=jax.ShapeDtypeStruct((b, s, FOX_W), BF16),
        compiler_params=_params(("parallel", "parallel", "arbitrary")),
        name="fox_attention",
    )(qi, kj, kind, z3, cqx, zm, zm, ckxm, z3, z3, ckx)


def _ret_kernel(q_ref, k_ref, v_ref, g_ref, km_ref, vm_ref, gn_ref, o_ref, state_sc, decay_sc,
                xi_sc, zeta_sc, cd_sc, *, heads):
    hg = pl.program_id(1)
    c = pl.program_id(2)
    chunk = q_ref.shape[0]
    dk, dv = RET_DK, RET_DV

    def log_gamma(head, width):
        hv = jnp.full((1, width), head, jnp.int32).astype(F32)
        return jnp.log1p(-jnp.exp2(-5.0 - hv))

    def rows(n, width):
        return lax.broadcasted_iota(jnp.int32, (n, width), 0).astype(F32)

    @pl.when(c == 0)
    def _():
        for g in range(heads):
            head = hg * heads + g
            diff = rows(chunk, chunk) - lax.broadcasted_iota(jnp.int32, (chunk, chunk), 1).astype(F32)
            decay_sc[g] = jnp.where(diff >= 0, jnp.exp(log_gamma(head, chunk) * jnp.maximum(diff, 0.0)), 0.0)
            xi_sc[g] = jnp.exp(log_gamma(head, dv) * (rows(chunk, dv) + 1.0))
            zeta_sc[g] = jnp.exp(log_gamma(head, dk) * (chunk - 1.0 - rows(chunk, dk)))
            cd_sc[g] = jnp.exp(log_gamma(head, dv) * float(chunk))
            zeta_m = jnp.exp(log_gamma(head, dk) * (BLOCK - 1.0 - rows(BLOCK, dk)))
            kz = (km_ref[:, g * dk:(g + 1) * dk].astype(F32) * zeta_m).astype(BF16)
            state_sc[g] = _tn(kz, vm_ref[:, g * dv:(g + 1) * dv])

    for g in range(heads):
        q = q_ref[:, g * dk:(g + 1) * dk]
        k = k_ref[:, g * dk:(g + 1) * dk]
        v = v_ref[:, g * dv:(g + 1) * dv]
        s = _nt(q, k) * decay_sc[g]
        intra = jnp.dot(s.astype(BF16), v, preferred_element_type=F32)
        state = state_sc[g]
        cross = jnp.dot(q, state.astype(BF16), preferred_element_type=F32) * xi_sc[g]
        o = intra + cross
        kz = (k.astype(F32) * zeta_sc[g]).astype(BF16)
        state_sc[g] = state * cd_sc[g] + _tn(kz, v)
        mu = jnp.mean(o, axis=-1, keepdims=True)
        d = o - mu
        var = jnp.mean(d * d, axis=-1, keepdims=True)
        y = d * lax.rsqrt(var + EPS) * gn_ref[:, g * dv:(g + 1) * dv]
        o_ref[:, g * dv:(g + 1) * dv] = (g_ref[:, g * dv:(g + 1) * dv].astype(F32) * y).astype(o_ref.dtype)


def _retention(z3, zm, ret_norm_g, *, chunk, heads):
    b, s, _ = z3.shape
    nc = s // chunk
    wk, wv = heads * RET_DK, heads * RET_DV
    rq, rk = Z_RQ // wk, Z_RK // wk
    rv, gr = Z_RV // wv, Z_GR // wv
    return pl.pallas_call(
        functools.partial(_ret_kernel, heads=heads),
        grid=(b, RET_HEADS // heads, nc),
        in_specs=[
            pl.BlockSpec((None, chunk, wk), lambda b_, h, c: (b_, c, rq + h)),
            pl.BlockSpec((None, chunk, wk), lambda b_, h, c: (b_, c, rk + h)),
            pl.BlockSpec((None, chunk, wv), lambda b_, h, c: (b_, c, rv + h)),
            pl.BlockSpec((None, chunk, wv), lambda b_, h, c: (b_, c, gr + h)),
            pl.BlockSpec((BLOCK, wk), lambda b_, h, c: (0, rk + h)),
            pl.BlockSpec((BLOCK, wv), lambda b_, h, c: (0, rv + h)),
            pl.BlockSpec((1, wv), lambda b_, h, c: (0, h)),
        ],
        out_specs=pl.BlockSpec((None, chunk, wv), lambda b_, h, c: (b_, c, h)),
        out_shape=jax.ShapeDtypeStruct((b, s, RET_V_W), BF16),
        scratch_shapes=[pltpu.VMEM((heads, RET_DK, RET_DV), F32),
                        pltpu.VMEM((heads, chunk, chunk), F32),
                        pltpu.VMEM((heads, chunk, RET_DV), F32),
                        pltpu.VMEM((heads, chunk, RET_DK), F32),
                        pltpu.VMEM((heads, 1, RET_DV), F32)],
        compiler_params=_params(("parallel", "parallel", "arbitrary")),
        name="retention",
    )(z3, z3, z3, z3, zm, zm, ret_norm_g.reshape(1, RET_V_W))


def _merge_kernel(ya_ref, yr_ref, wf_ref, wr_ref, ga_ref, gr_ref, o_ref):
    a = jnp.dot(ya_ref[...], wf_ref[...], preferred_element_type=F32)
    r = jnp.dot(yr_ref[...], wr_ref[...], preferred_element_type=F32)
    o_ref[...] = (ga_ref[...].astype(F32) * a + gr_ref[...].astype(F32) * r).astype(o_ref.dtype)


def _merge(ya, yr, wf, wr, z, *, tm, tn, d_model):
    m = ya.shape[0]
    ga0 = Z_GA // tn
    gr0 = (Z_GA + d_model) // tn
    return pl.pallas_call(
        _merge_kernel,
        grid=(m // tm, d_model // tn),
        in_specs=[pl.BlockSpec((tm, FOX_W), lambda i, j: (i, 0)),
                  pl.BlockSpec((tm, RET_V_W), lambda i, j: (i, 0)),
                  pl.BlockSpec((FOX_W, tn), lambda i, j: (0, j)),
                  pl.BlockSpec((RET_V_W, tn), lambda i, j: (0, j)),
                  pl.BlockSpec((tm, tn), lambda i, j: (i, ga0 + j)),
                  pl.BlockSpec((tm, tn), lambda i, j: (i, gr0 + j))],
        out_specs=pl.BlockSpec((tm, tn), lambda i, j: (i, j)),
        out_shape=jax.ShapeDtypeStruct((m, d_model), BF16),
        compiler_params=_params(("parallel", "arbitrary"), V7X_VMEM_BUDGET),
        name="merge_proj",
    )(ya, yr, wf, wr, z, z)


def _outproj_kernel(a_ref, w_ref, x_ref, o_ref):
    o_ref[...] = x_ref[...] + jnp.dot(a_ref[...], w_ref[...], preferred_element_type=F32)


def _outproj(a, w, x, *, tm, tn):
    m, k = a.shape
    n = w.shape[1]
    return pl.pallas_call(
        _outproj_kernel,
        grid=(m // tm, n // tn),
        in_specs=[pl.BlockSpec((tm, k), lambda i, j: (i, 0)),
                  pl.BlockSpec((k, tn), lambda i, j: (0, j)),
                  pl.BlockSpec((tm, tn), lambda i, j: (i, j))],
        out_specs=pl.BlockSpec((tm, tn), lambda i, j: (i, j)),
        out_shape=jax.ShapeDtypeStruct((m, n), F32),
        compiler_params=_params(("parallel", "arbitrary"), V7X_VMEM_BUDGET),
        name="out_proj",
    )(a, w, x)


def _peer_score_kernel(x_ref, w_ref, keys_ref, o_ref):
    acc = jnp.dot(x_ref[...], w_ref[...], preferred_element_type=F32)
    for g in range(acc.shape[1] // PEER_HALF):
        qg = acc[:, g * PEER_HALF:(g + 1) * PEER_HALF].astype(BF16)
        o_ref[g] = _nt(keys_ref[g], qg)


def _peer_scores(xn, wq, keys, *, tm, tn):
    m, d = xn.shape
    n = wq.shape[1]
    ng = tn // PEER_HALF
    return pl.pallas_call(
        _peer_score_kernel,
        grid=(m // tm, n // tn),
        in_specs=[pl.BlockSpec((tm, d), lambda i, j: (i, 0)),
                  pl.BlockSpec((d, tn), lambda i, j: (0, j)),
                  pl.BlockSpec((ng, PEER_NKEYS, PEER_HALF), lambda i, j: (j, 0, 0))],
        out_specs=pl.BlockSpec((ng, PEER_NKEYS, tm), lambda i, j: (j, 0, i)),
        out_shape=jax.ShapeDtypeStruct((n // PEER_HALF, PEER_NKEYS, m), F32),
        compiler_params=_params(("parallel", "arbitrary"), V7X_VMEM_BUDGET),
        name="peer_scores",
    )(xn, wq, keys)


def _top_values(x, n, stacked_rows):
    rows, width = x.shape
    idx = lax.broadcasted_iota(jnp.int32, (rows, width), 0).astype(F32)
    srow = lax.broadcasted_iota(jnp.int32, (stacked_rows, width), 0)
    stacked = jnp.zeros((stacked_rows, width), F32)
    rank = jnp.full((rows, width), float(n), F32)
    vals = []
    for t in range(n):
        cur = jnp.max(x, axis=0, keepdims=True)
        vals.append(cur)
        stacked = jnp.where(srow == t, cur, stacked)
        first = jnp.min(jnp.where(x == cur, idx, float(rows)), axis=0, keepdims=True)
        hit = idx == first
        rank = jnp.where(hit, float(t), rank)
        x = jnp.where(hit, -jnp.inf, x)
    return stacked, vals, rank


def _peer_topk_kernel(s_ref, e1_ref, e2_ref, rank_ref, cnt_ref):
    s1 = s_ref[0]
    s2 = s_ref[1]
    v1, v1l, rank1 = _top_values(s1, PEER_TOPK, PEER_TOPK)
    v2, v2l, _ = _top_values(s2, PEER_TOPK, PEER_TOPK)
    sub = lax.broadcasted_iota(jnp.int32, (8, s1.shape[1]), 0)
    pieces = [v1l[0] + v2]
    for a in range(1, 8):
        nb = PEER_TOPK // (a + 1)
        pieces.append(jnp.where(sub < nb, v1l[a] + v2[0:8], -jnp.inf))
    pieces.append(v1[8:16] + v2l[0])
    cand = jnp.concatenate(pieces, axis=0)
    _, top, _ = _top_values(cand, PEER_TOPK, 8)
    m = top[0]
    z = jnp.zeros_like(m)
    for t in range(PEER_TOPK):
        z = z + jnp.exp(top[t] - m)
    tau = top[PEER_TOPK - 1]
    cnt = jnp.zeros_like(s2)
    for a in range(PEER_TOPK):
        cnt = cnt + jnp.where(v1l[a] + s2 >= tau, 1.0, 0.0)
    rank_ref[...] = rank1
    cnt_ref[...] = cnt.astype(cnt_ref.dtype)
    e1_ref[...] = jnp.exp(s1 - v1l[0])
    e2_ref[...] = (jnp.exp(s2 - v2l[0]) / z).astype(e2_ref.dtype)


def _peer_topk(st, *, tt):
    _, _, t = st.shape
    spec = pl.BlockSpec((None, PEER_NKEYS, tt), lambda i, h: (h, 0, i))
    shape = lambda dt: jax.ShapeDtypeStruct((PEER_HEADS, PEER_NKEYS, t), dt)
    return pl.pallas_call(
        _peer_topk_kernel,
        grid=(t // tt, PEER_HEADS),
        in_specs=[pl.BlockSpec((2, PEER_NKEYS, tt), lambda i, h: (h, 0, i))],
        out_specs=[spec, spec, spec, spec],
        out_shape=[shape(F32), shape(BF16), shape(F32), shape(BF16)],
        compiler_params=_params(("parallel", "parallel")),
        name="peer_topk",
    )(st)


def _gelu_exact(x):
    return 0.5 * x * (1.0 + lax.erf(x * (2.0 ** -0.5)))


def _peer_dense_kernel(xt_ref, u_ref, v_ref, e1_ref, rank_ref, e2_ref, cnt_ref, h_ref, o_ref, gate_sc, *,
                       na, halves):
    j = pl.program_id(1)

    def build_gate(chunk, gate_sc):
        sub = 16
        bshape = (sub, V7X_LANES)
        for a in range(na):
            row = chunk * na + a
            rkrows = [rank_ref[h, pl.ds(row, 1), :] for h in range(PEER_HEADS)]
            e1rows = [e1_ref[h, pl.ds(row, 1), :] for h in range(PEER_HEADS)]
            for l0 in range(0, gate_sc.shape[1], V7X_LANES):
                ls = slice(l0, l0 + V7X_LANES)
                rkb = [jnp.broadcast_to(rkrows[h][:, ls], bshape).astype(BF16) for h in range(PEER_HEADS)]
                e1b = [jnp.broadcast_to(e1rows[h][:, ls], bshape).astype(BF16) for h in range(PEER_HEADS)]
                for r in range(0, PEER_NKEYS, sub):
                    gate = None
                    for h in range(PEER_HEADS):
                        sel = rkb[h] < cnt_ref[h, r:r + sub, ls]
                        term = jnp.where(sel, e1b[h] * e2_ref[h, r:r + sub, ls], jnp.zeros(bshape, BF16))
                        gate = term if gate is None else gate + term
                    gate_sc[a * PEER_NKEYS + r:a * PEER_NKEYS + r + sub, ls] = gate

    @pl.when(j == 0)
    def _():
        o_ref[...] = h_ref[...]

    build_gate(j, gate_sc)
    rows = na * PEER_NKEYS // halves
    total = None
    for k in range(halves):
        sl = slice(k * rows, (k + 1) * rows)
        act = jnp.dot(u_ref[sl, :], xt_ref[...], preferred_element_type=F32)
        w = gate_sc[sl, :] * _gelu_exact(act).astype(BF16)
        part = _tn(w, v_ref[sl, :])
        total = part if total is None else total + part
    o_ref[...] += total


def _peer_dense(xnt, u_tab, v_tab, e1, rank1, e2, cnt, h, *, tt, ec):
    d, t = xnt.shape
    n_exp = u_tab.shape[0]
    na = ec // PEER_NKEYS
    once = pl.Buffered(1)
    kern = functools.partial(_peer_dense_kernel, na=na, halves=2)
    table = pl.BlockSpec((PEER_HEADS, PEER_NKEYS, tt), lambda i, j: (0, 0, i), pipeline_mode=once)
    return pl.pallas_call(
        kern,
        grid=(t // tt, n_exp // ec),
        in_specs=[pl.BlockSpec((d, tt), lambda i, j: (0, i), pipeline_mode=once),
                  pl.BlockSpec((ec, d), lambda i, j: (j, 0)),
                  pl.BlockSpec((ec, d), lambda i, j: (j, 0)),
                  table, table, table, table,
                  pl.BlockSpec((tt, d), lambda i, j: (i, 0), pipeline_mode=once)],
        out_specs=pl.BlockSpec((tt, d), lambda i, j: (i, 0)),
        out_shape=jax.ShapeDtypeStruct((t, d), F32),
        scratch_shapes=[pltpu.VMEM((ec, tt), BF16)],
        compiler_params=_params(("parallel", "arbitrary"), V7X_VMEM_BYTES - 4 * 1024 * 1024),
        name="peer_dense",
    )(xnt, u_tab, v_tab, e1, rank1, e2, cnt, h)


def _rotary_tables(pos, valid):
    half = RET_DK // 2
    inv = ROPE_BASE ** (-jnp.arange(half, dtype=F32) / half)
    ang = pos[:, None] * inv[None, :]
    cos, sin = jnp.cos(ang), jnp.sin(ang)
    c = jnp.concatenate([cos, cos], axis=-1)
    s = jnp.concatenate([-sin, sin], axis=-1)
    kscale = (RET_DK ** -0.5) * valid.astype(F32)[:, None]
    return jnp.stack([c, c * kscale]), jnp.stack([s, s * kscale])


def _tile(n, pref):
    t = min(n, pref)
    assert n % t == 0
    return t


def kernel(x, meta_tokens, norm_mix_g, w_in, b_forget, q_norm_g, k_norm_g, ret_norm_g, w_proj_fox,
           w_proj_ret, w_out, norm_ffn_g, peer_w_q, peer_keys_1, peer_keys_2, peer_u, peer_v):
    b, s, d = x.shape
    depth = w_in.shape[0]
    assert depth == 1, "only the real-token outputs of a single layer are computed"
    m = b * s
    xr = x.reshape(m, d)
    hm = jnp.concatenate([jnp.zeros((PAD, d), x.dtype), meta_tokens.astype(x.dtype)], axis=0)

    tm = _tile(m, 1024)
    tn = _tile(d, 1024)
    wt = jnp.swapaxes(w_in[0], 0, 1)
    o_fq, o_fk, o_fv, o_f = 0, FOX_W, 2 * FOX_W, 3 * FOX_W
    o_rq = o_f + FOX_HEADS
    o_rk, o_rv = o_rq + RET_QK_W, o_rq + 2 * RET_QK_W
    o_gr = o_rv + RET_V_W
    o_ga = o_gr + RET_V_W
    o_gb = o_ga + d
    rows = lambda o, n: wt[o:o + n]
    wpt = jnp.concatenate([rows(o_fq, FOX_W), rows(o_fk, FOX_W), rows(o_rq, RET_QK_W), rows(o_rk, RET_QK_W),
                           rows(o_fv, FOX_W), rows(o_rv, RET_V_W), rows(o_gr, RET_V_W), rows(o_ga, d),
                           rows(o_gb, d)], axis=0).astype(BF16)
    npk = wpt.shape[0]
    wft = jnp.pad(rows(o_f, FOX_HEADS), ((0, V7X_LANES - FOX_HEADS), (0, 0))).astype(BF16)
    bf = jnp.pad(b_forget[0], (0, V7X_LANES - FOX_HEADS)).reshape(1, V7X_LANES)
    gain = jnp.concatenate([jnp.tile(q_norm_g[0] * (FOX_HD ** -0.5 * LOG2_E), FOX_HEADS),
                            jnp.tile(k_norm_g[0], FOX_HEADS),
                            jnp.zeros((npk - 2 * FOX_W,), F32)]).reshape(1, npk)

    pos_real = jnp.arange(s, dtype=F32) + N_META
    rc, rs = _rotary_tables(pos_real, jnp.ones((s,), bool))
    pos_meta = jnp.arange(BLOCK, dtype=F32) - PAD
    rcm, rsm = _rotary_tables(pos_meta, jnp.arange(BLOCK) >= PAD)

    u = _rmsnorm_bf16(xr, norm_mix_g[0], _tile(m, 512))
    um = _rmsnorm_bf16(hm, norm_mix_g[0], BLOCK)
    z = _proj(u, wpt, gain, rc, rs, tm=_tile(s, 1024), tn=tn, d_model=d, first_valid_row=0)
    zm = _proj(um, wpt, gain, rcm, rsm, tm=BLOCK, tn=tn, d_model=d, first_valid_row=PAD)
    cm, _, ckxm = _forget_cumsum(um, wft, bf, jnp.zeros((1, V7X_LANES), F32), nb=1, tm=BLOCK,
                                 first_valid_row=PAD)
    _, cqx, ckx = _forget_cumsum(u, wft, bf, cm[BLOCK - 1:BLOCK], nb=b, tm=_tile(s, 512), first_valid_row=0)

    z3 = z.reshape(b, s, npk)
    ya = _fox_attention(z3, zm, cqx.reshape(b, s, V7X_LANES), ckx.reshape(b, s, FOX_W), ckxm,
                        tq=_tile(s, 512), heads=16)
    chunk = _tile(s, 256)
    yr = _retention(z3, zm, ret_norm_g[0], chunk=chunk, heads=8)

    merged = _merge(ya.reshape(m, FOX_W), yr.reshape(m, RET_V_W), w_proj_fox[0].astype(BF16),
                    w_proj_ret[0].astype(BF16), z, tm=tm, tn=_tile(d, 512), d_model=d)
    h1 = _outproj(merged, w_out[0].astype(BF16), xr, tm=tm, tn=tn)

    xn, xnt = _rmsnorm_bf16(h1, norm_ffn_g[0], _tile(m, 512), with_transpose=True)
    keys = jnp.stack([peer_keys_1[0], peer_keys_2[0]], axis=1).reshape(
        2 * PEER_HEADS, PEER_NKEYS, PEER_HALF).astype(BF16)
    st = _peer_scores(xn, peer_w_q[0].astype(BF16), keys, tm=tm, tn=_tile(PEER_HEADS * PEER_QDIM, 1024))
    e1, e2, rank1, cnt = _peer_topk(st, tt=_tile(m, 512))
    out = _peer_dense(xnt, peer_u[0].astype(BF16), peer_v[0].astype(BF16), e1, rank1, e2, cnt, h1,
                      tt=_tile(m, 512), ec=512)
    return out.reshape(b, s, d)
```

```python
import functools

import jax
import jax.numpy as jnp
from jax import lax
from jax.experimental import pallas as pl
from jax.experimental.pallas import tpu as pltpu

N_META = 16
BLOCK = 128
PAD = BLOCK - N_META
EPS = 1e-6
NEG = -1e30
LOG2_E = 1.4426950408889634
FOX_HEADS = 16
FOX_HD = 128
FOX_W = FOX_HEADS * FOX_HD
RET_HEADS = 16
RET_DK = 128
RET_DV = 256
RET_QK_W = RET_HEADS * RET_DK
RET_V_W = RET_HEADS * RET_DV
ROPE_BASE = 10000.0
PEER_HEADS = 8
PEER_NKEYS = 128
PEER_N = PEER_NKEYS * PEER_NKEYS
PEER_QDIM = 256
PEER_HALF = PEER_QDIM // 2
PEER_TOPK = 16

V7X_LANES = 128
V7X_VMEM_BYTES = 64 * 1024 * 1024
V7X_VMEM_BUDGET = 56 * 1024 * 1024

BF16 = jnp.bfloat16
F32 = jnp.float32

Z_FQ = 0
Z_FK = Z_FQ + FOX_W
Z_RQ = Z_FK + FOX_W
Z_RK = Z_RQ + RET_QK_W
Z_FV = Z_RK + RET_QK_W
Z_RV = Z_FV + FOX_W
Z_GR = Z_RV + RET_V_W
Z_GA = Z_GR + RET_V_W


def _params(sem, vmem=None):
    return pltpu.CompilerParams(dimension_semantics=sem, vmem_limit_bytes=vmem)


def _nt(a, b):
    return lax.dot_general(a, b, (((1,), (1,)), ((), ())), preferred_element_type=F32)


def _tn(a, b):
    return lax.dot_general(a, b, (((0,), (0,)), ((), ())), preferred_element_type=F32)


def _sigmoid(x):
    return 0.5 + 0.5 * jnp.tanh(0.5 * x)


def _rmsnorm_kernel(x_ref, g_ref, o_ref, *ot_ref):
    x = x_ref[...]
    ms = jnp.mean(x * x, axis=-1, keepdims=True)
    y = (x * lax.rsqrt(ms + EPS) * g_ref[...]).astype(o_ref.dtype)
    o_ref[...] = y
    if ot_ref:
        ot_ref[0][...] = y.T


def _rmsnorm_bf16(x, g, tm, with_transpose=False):
    m, d = x.shape
    out_specs = [pl.BlockSpec((tm, d), lambda i: (i, 0))]
    out_shape = [jax.ShapeDtypeStruct((m, d), BF16)]
    if with_transpose:
        out_specs.append(pl.BlockSpec((d, tm), lambda i: (0, i)))
        out_shape.append(jax.ShapeDtypeStruct((d, m), BF16))
    out = pl.pallas_call(
        _rmsnorm_kernel,
        grid=(m // tm,),
        in_specs=[pl.BlockSpec((tm, d), lambda i: (i, 0)),
                  pl.BlockSpec((1, d), lambda i: (0, 0))],
        out_specs=out_specs,
        out_shape=out_shape,
        compiler_params=_params(("parallel",), V7X_VMEM_BUDGET),
        name="rmsnorm_bf16",
    )(x, g.reshape(1, d))
    return out if with_transpose else out[0]


def _proj_kernel(src_ref, u_ref, w_ref, gain_ref, rc_ref, rs_ref, o_ref, *, classes, first_valid_row):
    del src_ref
    j = pl.program_id(1)
    acc = _nt(u_ref[...], w_ref[...])
    tm, tn = acc.shape
    groups = [slice(g * V7X_LANES, (g + 1) * V7X_LANES) for g in range(tn // V7X_LANES)]

    def emit(kind):
        if kind == "headnorm":
            for sl in groups:
                x = acc[:, sl]
                ms = jnp.mean(x * x, axis=-1, keepdims=True)
                o_ref[:, sl] = (x * lax.rsqrt(ms + EPS) * gain_ref[:, sl]).astype(o_ref.dtype)
        elif kind == "rotary":
            c = rc_ref[...]
            s = rs_ref[...]
            for sl in groups:
                x = acc[:, sl]
                o_ref[:, sl] = (x * c + pltpu.roll(x, V7X_LANES // 2, 1) * s).astype(o_ref.dtype)
        elif kind == "plain":
            o_ref[...] = acc.astype(o_ref.dtype)
        elif kind == "vmasked":
            if first_valid_row > 0:
                rows = lax.broadcasted_iota(jnp.int32, (tm, 1), 0) + pl.program_id(0) * tm
                o_ref[...] = jnp.where(rows >= first_valid_row, acc, 0.0).astype(o_ref.dtype)
            else:
                o_ref[...] = acc.astype(o_ref.dtype)
        elif kind == "silu":
            o_ref[...] = (acc * _sigmoid(acc)).astype(o_ref.dtype)
        elif kind == "sigmoid":
            o_ref[...] = _sigmoid(acc).astype(o_ref.dtype)
        else:
            raise ValueError(kind)

    for kind, lo, hi in classes:
        pl.when((j >= lo) & (j < hi))(functools.partial(emit, kind))


def _proj(u, wt, src_rows, gain, rot_c, rot_s, *, tm, tn, d_model, first_valid_row):
    m, d = u.shape
    npk = src_rows.shape[0] * tn
    n_row_tiles = rot_c.shape[1] // tm
    widths = (("headnorm", 2 * FOX_W), ("rotary", 2 * RET_QK_W), ("plain", FOX_W),
              ("vmasked", RET_V_W), ("silu", RET_V_W), ("sigmoid", 2 * d_model))
    classes, lo = [], 0
    for kind, w in widths:
        classes.append((kind, lo, lo + w // tn))
        lo += w // tn
    assert lo * tn == npk
    rk_lo = Z_RK // tn
    kern = functools.partial(_proj_kernel, classes=tuple(classes), first_valid_row=first_valid_row)
    rot_map = lambda i, j, src: ((j >= rk_lo).astype(jnp.int32), i % n_row_tiles, 0)
    grid_spec = pltpu.PrefetchScalarGridSpec(
        num_scalar_prefetch=1,
        grid=(m // tm, npk // tn),
        in_specs=[pl.BlockSpec((tm, d), lambda i, j, src: (i, 0)),
                  pl.BlockSpec((pl.Element(tn), pl.Element(d)),
                               lambda i, j, src: (pl.multiple_of(src[j], FOX_HEADS), 0)),
                  pl.BlockSpec((1, tn), lambda i, j, src: (0, j)),
                  pl.BlockSpec((None, tm, V7X_LANES), rot_map),
                  pl.BlockSpec((None, tm, V7X_LANES), rot_map)],
        out_specs=pl.BlockSpec((tm, tn), lambda i, j, src: (i, j)),
    )
    return pl.pallas_call(
        kern,
        grid_spec=grid_spec,
        out_shape=jax.ShapeDtypeStruct((m, npk), BF16),
        compiler_params=_params(("parallel", "arbitrary"), V7X_VMEM_BUDGET),
        name="input_proj",
    )(src_rows, u, wt, gain, rot_c, rot_s)


def _split3(x):
    hi = x.astype(BF16)
    r = x - hi.astype(F32)
    mid = r.astype(BF16)
    lo = (r - mid.astype(F32)).astype(BF16)
    return hi, mid, lo


def _forget_kernel(u_ref, w_ref, b_ref, cin_ref, selq_ref, selk_ref, o_ref, cqx_ref, ckx_ref, carry_ref, *,
                   first_valid_row):
    i = pl.program_id(1)

    @pl.when(i == 0)
    def _():
        carry_ref[...] = cin_ref[...]

    x = _nt(u_ref[...], w_ref[...]) + b_ref[...]
    tm = x.shape[0]
    logf = jnp.minimum(x, 0.0) - jnp.log1p(jnp.exp(-jnp.abs(x)))
    rows = lax.broadcasted_iota(jnp.int32, (tm, tm), 0)
    cols = lax.broadcasted_iota(jnp.int32, (tm, tm), 1)
    if first_valid_row > 0:
        r1 = lax.broadcasted_iota(jnp.int32, (tm, 1), 0) + i * tm
        logf = jnp.where(r1 >= first_valid_row, logf, 0.0)
    tri = jnp.where(rows >= cols, 1.0, 0.0).astype(BF16)
    hi, mid, lo = _split3(logf)
    cs = (jnp.dot(tri, hi, preferred_element_type=F32)
          + jnp.dot(tri, mid, preferred_element_type=F32)
          + jnp.dot(tri, lo, preferred_element_type=F32))
    c = cs + carry_ref[...]
    o_ref[...] = c
    carry_ref[...] = c[tm - 1:tm, :]
    chi, cmid, clo = _split3(c * LOG2_E)
    lane = lax.broadcasted_iota(jnp.int32, c.shape, 1)
    one = jnp.where(lane < FOX_HEADS, 1.0, 0.0).astype(BF16)
    pq = jnp.concatenate([chi, cmid, clo, one, one, one], axis=1)
    pk = jnp.concatenate([one, one, one, -chi, -cmid, -clo], axis=1)
    cqx_ref[...] = jnp.dot(pq, selq_ref[...], preferred_element_type=F32).astype(BF16)
    ckx_ref[...] = jnp.dot(pk, selk_ref[...], preferred_element_type=F32).astype(BF16)


FOX_EXT = 6


def _forget_selectors():
    import numpy as np
    selq = np.zeros((FOX_EXT * V7X_LANES, V7X_LANES), np.float32)
    selk = np.zeros((FOX_EXT * V7X_LANES, FOX_W), np.float32)
    for p in range(FOX_EXT):
        for h in range(FOX_HEADS):
            selq[p * V7X_LANES + h, FOX_EXT * h + p] = 1.0
            selk[p * V7X_LANES + h, FOX_HD * h + FOX_EXT * h + p] = 1.0
    return jnp.asarray(selq, BF16), jnp.asarray(selk, BF16)


def _forget_cumsum(u, wf, bf, carry_in, *, nb, tm, first_valid_row):
    m, d = u.shape
    nt = m // (nb * tm)
    selq, selk = _forget_selectors()
    kern = functools.partial(_forget_kernel, first_valid_row=first_valid_row)
    const = lambda b, i: (0, 0)
    row = lambda b, i: (b * nt + i, 0)
    return pl.pallas_call(
        kern,
        grid=(nb, nt),
        in_specs=[pl.BlockSpec((tm, d), row),
                  pl.BlockSpec((V7X_LANES, d), const),
                  pl.BlockSpec((1, V7X_LANES), const),
                  pl.BlockSpec((1, V7X_LANES), const),
                  pl.BlockSpec(selq.shape, const),
                  pl.BlockSpec(selk.shape, const)],
        out_specs=[pl.BlockSpec((tm, V7X_LANES), row),
                   pl.BlockSpec((tm, V7X_LANES), row),
                   pl.BlockSpec((tm, FOX_W), row)],
        out_shape=[jax.ShapeDtypeStruct((m, V7X_LANES), F32),
                   jax.ShapeDtypeStruct((m, V7X_LANES), BF16),
                   jax.ShapeDtypeStruct((m, FOX_W), BF16)],
        scratch_shapes=[pltpu.VMEM((1, V7X_LANES), F32)],
        compiler_params=_params(("parallel", "arbitrary"), V7X_VMEM_BUDGET),
        name="forget_cumsum",
    )(u, wf, bf, carry_in, selq, selk)


def _fox_kernel(qi_ref, kj_ref, kind_ref, q_ref, cqx_ref, km_ref, vm_ref, ckm_ref, k_ref, v_ref,
                ck_ref, o_ref, m_sc, acc_sc, qx_sc, *, heads):
    t = pl.program_id(2)
    kind = kind_ref[t]
    hd = FOX_HD
    lanes = [slice(g * hd, (g + 1) * hd) for g in range(heads)]

    def scores(g, k_r, ck_r):
        sl = lanes[g]
        k_ext = jnp.concatenate([k_r[:, sl], ck_r[:, sl]], axis=1)
        return _nt(k_ext, qx_sc[g])

    def pv(p, v):
        ones = jnp.ones((16, v.shape[0]), v.dtype)
        return jnp.dot(jnp.concatenate([v.T, ones], axis=0), p, preferred_element_type=F32)

    def update(g, s, v):
        m_prev = m_sc[g]
        m_new = jnp.maximum(m_prev, jnp.max(s, axis=0, keepdims=True))
        alpha = jnp.exp2(m_prev - m_new)
        p = jnp.exp2(s - m_new).astype(BF16)
        acc_sc[g] = alpha * acc_sc[g] + pv(p, v)
        m_sc[g] = m_new

    @pl.when(kind == 0)
    def _():
        for g, sl in enumerate(lanes):
            qx_sc[g] = jnp.concatenate([q_ref[:, sl], cqx_ref[...]], axis=1)
            s = scores(g, km_ref, ckm_ref)
            kpos = lax.broadcasted_iota(jnp.int32, s.shape, 0)
            s = jnp.where(kpos >= PAD, s, NEG)
            m = jnp.max(s, axis=0, keepdims=True)
            p = jnp.exp2(s - m).astype(BF16)
            m_sc[g] = m
            acc_sc[g] = pv(p, vm_ref[:, sl])

    @pl.when(kind == 1)
    def _():
        for g, sl in enumerate(lanes):
            update(g, scores(g, k_ref, ck_ref), v_ref[:, sl])

    @pl.when(kind == 2)
    def _():
        for g, sl in enumerate(lanes):
            s = scores(g, k_ref, ck_ref)
            krow = lax.broadcasted_iota(jnp.int32, s.shape, 0)
            qcol = lax.broadcasted_iota(jnp.int32, s.shape, 1)
            s = jnp.where(krow <= qcol, s, NEG)
            update(g, s, v_ref[:, sl])
            acc = acc_sc[g]
            o_ref[:, sl] = (acc[:hd, :] / acc[hd:hd + 1, :]).T.astype(o_ref.dtype)


def _fox_attention(z3, zm, cqx, ckx, ckxm, *, tq, heads):
    b, s, _ = z3.shape
    nq = s // tq
    qi, kj, kind = [], [], []
    for i in range(nq):
        qi += [i] * (i + 2)
        kj += [0] + list(range(i + 1))
        kind += [0] + [1] * i + [2]
    n_steps = len(qi)
    qi = jnp.asarray(qi, jnp.int32)
    kj = jnp.asarray(kj, jnp.int32)
    kind = jnp.asarray(kind, jnp.int32)
    w = heads * FOX_HD
    fq, fk, fv = Z_FQ // w, Z_FK // w, Z_FV // w
    grid_spec = pltpu.PrefetchScalarGridSpec(
        num_scalar_prefetch=3,
        grid=(b, FOX_HEADS // heads, n_steps),
        in_specs=[
            pl.BlockSpec((None, tq, w), lambda b_, h, t, qi, kj, kd: (b_, qi[t], fq + h)),
            pl.BlockSpec((None, tq, V7X_LANES), lambda b_, h, t, qi, kj, kd: (b_, qi[t], 0)),
            pl.BlockSpec((BLOCK, w), lambda b_, h, t, qi, kj, kd: (0, fk + h)),
            pl.BlockSpec((BLOCK, w), lambda b_, h, t, qi, kj, kd: (0, fv + h)),
            pl.BlockSpec((BLOCK, w), lambda b_, h, t, qi, kj, kd: (0, h)),
            pl.BlockSpec((None, tq, w), lambda b_, h, t, qi, kj, kd: (b_, kj[t], fk + h)),
            pl.BlockSpec((None, tq, w), lambda b_, h, t, qi, kj, kd: (b_, kj[t], fv + h)),
            pl.BlockSpec((None, tq, w), lambda b_, h, t, qi, kj, kd: (b_, kj[t], h)),
        ],
        out_specs=pl.BlockSpec((None, tq, w), lambda b_, h, t, qi, kj, kd: (b_, qi[t], h)),
        scratch_shapes=[pltpu.VMEM((heads, 1, tq), F32),
                        pltpu.VMEM((heads, FOX_HD + 16, tq), F32),
                        pltpu.VMEM((heads, tq, 2 * FOX_HD), BF16)],
    )
    return pl.pallas_call(
        functools.partial(_fox_kernel, heads=heads),
        grid_spec=grid_spec,
        out_shape=jax.ShapeDtypeStruct((b, s, FOX_W), BF16),
        compiler_params=_params(("parallel", "parallel", "arbitrary")),
        name="fox_attention",
    )(qi, kj, kind, z3, cqx, zm, zm, ckxm, z3, z3, ckx)


def _ret_kernel(q_ref, k_ref, v_ref, g_ref, km_ref, vm_ref, gn_ref, o_ref, state_sc, decay_sc,
                xi_sc, zeta_sc, cd_sc, *, heads):
    hg = pl.program_id(1)
    c = pl.program_id(2)
    chunk = q_ref.shape[0]
    dk, dv = RET_DK, RET_DV

    def log_gamma(head, width):
        hv = jnp.full((1, width), head, jnp.int32).astype(F32)
        return jnp.log1p(-jnp.exp2(-5.0 - hv))

    def rows(n, width):
        return lax.broadcasted_iota(jnp.int32, (n, width), 0).astype(F32)

    @pl.when(c == 0)
    def _():
        for g in range(heads):
            head = hg * heads + g
            diff = rows(chunk, chunk) - lax.broadcasted_iota(jnp.int32, (chunk, chunk), 1).astype(F32)
            decay_sc[g] = jnp.where(diff >= 0, jnp.exp(log_gamma(head, chunk) * jnp.maximum(diff, 0.0)), 0.0)
            xi_sc[g] = jnp.exp(log_gamma(head, dv) * (rows(chunk, dv) + 1.0))
            zeta_sc[g] = jnp.exp(log_gamma(head, dk) * (chunk - 1.0 - rows(chunk, dk)))
            cd_sc[g] = jnp.exp(log_gamma(head, dv) * float(chunk))
            zeta_m = jnp.exp(log_gamma(head, dk) * (BLOCK - 1.0 - rows(BLOCK, dk)))
            kz = (km_ref[:, g * dk:(g + 1) * dk].astype(F32) * zeta_m).astype(BF16)
            state_sc[g] = _tn(kz, vm_ref[:, g * dv:(g + 1) * dv])

    for g in range(heads):
        q = q_ref[:, g * dk:(g + 1) * dk]
        k = k_ref[:, g * dk:(g + 1) * dk]
        v = v_ref[:, g * dv:(g + 1) * dv]
        s = _nt(q, k) * decay_sc[g]
        intra = jnp.dot(s.astype(BF16), v, preferred_element_type=F32)
        state = state_sc[g]
        cross = jnp.dot(q, state.astype(BF16), preferred_element_type=F32) * xi_sc[g]
        o = intra + cross
        kz = (k.astype(F32) * zeta_sc[g]).astype(BF16)
        state_sc[g] = state * cd_sc[g] + _tn(kz, v)
        mu = jnp.mean(o, axis=-1, keepdims=True)
        d = o - mu
        var = jnp.mean(d * d, axis=-1, keepdims=True)
        y = d * lax.rsqrt(var + EPS) * gn_ref[:, g * dv:(g + 1) * dv]
        o_ref[:, g * dv:(g + 1) * dv] = (g_ref[:, g * dv:(g + 1) * dv].astype(F32) * y).astype(o_ref.dtype)


def _retention(z3, zm, ret_norm_g, *, chunk, heads):
    b, s, _ = z3.shape
    nc = s // chunk
    wk, wv = heads * RET_DK, heads * RET_DV
    rq, rk = Z_RQ // wk, Z_RK // wk
    rv, gr = Z_RV // wv, Z_GR // wv
    return pl.pallas_call(
        functools.partial(_ret_kernel, heads=heads),
        grid=(b, RET_HEADS // heads, nc),
        in_specs=[
            pl.BlockSpec((None, chunk, wk), lambda b_, h, c: (b_, c, rq + h)),
            pl.BlockSpec((None, chunk, wk), lambda b_, h, c: (b_, c, rk + h)),
            pl.BlockSpec((None, chunk, wv), lambda b_, h, c: (b_, c, rv + h)),
            pl.BlockSpec((None, chunk, wv), lambda b_, h, c: (b_, c, gr + h)),
            pl.BlockSpec((BLOCK, wk), lambda b_, h, c: (0, rk + h)),
            pl.BlockSpec((BLOCK, wv), lambda b_, h, c: (0, rv + h)),
            pl.BlockSpec((1, wv), lambda b_, h, c: (0, h)),
        ],
        out_specs=pl.BlockSpec((None, chunk, wv), lambda b_, h, c: (b_, c, h)),
        out_shape=jax.ShapeDtypeStruct((b, s, RET_V_W), BF16),
        scratch_shapes=[pltpu.VMEM((heads, RET_DK, RET_DV), F32),
                        pltpu.VMEM((heads, chunk, chunk), F32),
                        pltpu.VMEM((heads, chunk, RET_DV), F32),
                        pltpu.VMEM((heads, chunk, RET_DK), F32),
                        pltpu.VMEM((heads, 1, RET_DV), F32)],
        compiler_params=_params(("parallel", "parallel", "arbitrary")),
        name="retention",
    )(z3, z3, z3, z3, zm, zm, ret_norm_g.reshape(1, RET_V_W))


def _merge_kernel(ya_ref, yr_ref, wf_ref, wr_ref, ga_ref, gr_ref, o_ref):
    a = jnp.dot(ya_ref[...], wf_ref[...], preferred_element_type=F32)
    r = jnp.dot(yr_ref[...], wr_ref[...], preferred_element_type=F32)
    o_ref[...] = (ga_ref[...].astype(F32) * a + gr_ref[...].astype(F32) * r).astype(o_ref.dtype)


def _merge(ya, yr, wf, wr, z, *, tm, tn, d_model):
    m = ya.shape[0]
    ga0 = Z_GA // tn
    gr0 = (Z_GA + d_model) // tn
    return pl.pallas_call(
        _merge_kernel,
        grid=(m // tm, d_model // tn),
        in_specs=[pl.BlockSpec((tm, FOX_W), lambda i, j: (i, 0)),
                  pl.BlockSpec((tm, RET_V_W), lambda i, j: (i, 0)),
                  pl.BlockSpec((FOX_W, tn), lambda i, j: (0, j)),
                  pl.BlockSpec((RET_V_W, tn), lambda i, j: (0, j)),
                  pl.BlockSpec((tm, tn), lambda i, j: (i, ga0 + j)),
                  pl.BlockSpec((tm, tn), lambda i, j: (i, gr0 + j))],
        out_specs=pl.BlockSpec((tm, tn), lambda i, j: (i, j)),
        out_shape=jax.ShapeDtypeStruct((m, d_model), BF16),
        compiler_params=_params(("parallel", "arbitrary"), V7X_VMEM_BUDGET),
        name="merge_proj",
    )(ya, yr, wf, wr, z, z)


def _outproj_kernel(a_ref, w_ref, x_ref, o_ref):
    o_ref[...] = x_ref[...] + jnp.dot(a_ref[...], w_ref[...], preferred_element_type=F32)


def _outproj(a, w, x, *, tm, tn):
    m, k = a.shape
    n = w.shape[1]
    return pl.pallas_call(
        _outproj_kernel,
        grid=(m // tm, n // tn),
        in_specs=[pl.BlockSpec((tm, k), lambda i, j: (i, 0)),
                  pl.BlockSpec((k, tn), lambda i, j: (0, j)),
                  pl.BlockSpec((tm, tn), lambda i, j: (i, j))],
        out_specs=pl.BlockSpec((tm, tn), lambda i, j: (i, j)),
        out_shape=jax.ShapeDtypeStruct((m, n), F32),
        compiler_params=_params(("parallel", "arbitrary"), V7X_VMEM_BUDGET),
        name="out_proj",
    )(a, w, x)


def _peer_score_kernel(x_ref, w_ref, keys_ref, o_ref):
    acc = jnp.dot(x_ref[...], w_ref[...], preferred_element_type=F32)
    for g in range(acc.shape[1] // PEER_HALF):
        qg = acc[:, g * PEER_HALF:(g + 1) * PEER_HALF].astype(BF16)
        o_ref[g] = _nt(keys_ref[g], qg)


def _peer_scores(xn, wq, keys, *, tm, tn):
    m, d = xn.shape
    n = wq.shape[1]
    ng = tn // PEER_HALF
    return pl.pallas_call(
        _peer_score_kernel,
        grid=(m // tm, n // tn),
        in_specs=[pl.BlockSpec((tm, d), lambda i, j: (i, 0)),
                  pl.BlockSpec((d, tn), lambda i, j: (0, j)),
                  pl.BlockSpec((ng, PEER_NKEYS, PEER_HALF), lambda i, j: (j, 0, 0))],
        out_specs=pl.BlockSpec((ng, PEER_NKEYS, tm), lambda i, j: (j, 0, i)),
        out_shape=jax.ShapeDtypeStruct((n // PEER_HALF, PEER_NKEYS, m), F32),
        compiler_params=_params(("parallel", "arbitrary"), V7X_VMEM_BUDGET),
        name="peer_scores",
    )(xn, wq, keys)


def _top_values(x, n, stacked_rows):
    rows, width = x.shape
    idx = lax.broadcasted_iota(jnp.int32, (rows, width), 0).astype(F32)
    srow = lax.broadcasted_iota(jnp.int32, (stacked_rows, width), 0)
    stacked = jnp.zeros((stacked_rows, width), F32)
    rank = jnp.full((rows, width), float(n), F32)
    vals = []
    for t in range(n):
        cur = jnp.max(x, axis=0, keepdims=True)
        vals.append(cur)
        stacked = jnp.where(srow == t, cur, stacked)
        first = jnp.min(jnp.where(x == cur, idx, float(rows)), axis=0, keepdims=True)
        hit = idx == first
        rank = jnp.where(hit, float(t), rank)
        x = jnp.where(hit, -jnp.inf, x)
    return stacked, vals, rank


def _peer_topk_kernel(s_ref, e1_ref, e2_ref, rank_ref, cnt_ref):
    s1 = s_ref[0]
    s2 = s_ref[1]
    v1, v1l, rank1 = _top_values(s1, PEER_TOPK, PEER_TOPK)
    v2, v2l, _ = _top_values(s2, PEER_TOPK, PEER_TOPK)
    sub = lax.broadcasted_iota(jnp.int32, (8, s1.shape[1]), 0)
    pieces = [v1l[0] + v2]
    for a in range(1, 8):
        nb = PEER_TOPK // (a + 1)
        pieces.append(jnp.where(sub < nb, v1l[a] + v2[0:8], -jnp.inf))
    pieces.append(v1[8:16] + v2l[0])
    cand = jnp.concatenate(pieces, axis=0)
    _, top, _ = _top_values(cand, PEER_TOPK, 8)
    m = top[0]
    z = jnp.zeros_like(m)
    for t in range(PEER_TOPK):
        z = z + jnp.exp(top[t] - m)
    tau = top[PEER_TOPK - 1]
    cnt = jnp.zeros_like(s2)
    for a in range(PEER_TOPK):
        cnt = cnt + jnp.where(v1l[a] + s2 >= tau, 1.0, 0.0)
    rank_ref[...] = rank1
    cnt_ref[...] = cnt.astype(cnt_ref.dtype)
    e1_ref[...] = jnp.exp(s1 - v1l[0])
    e2_ref[...] = (jnp.exp(s2 - v2l[0]) / z).astype(e2_ref.dtype)


def _peer_topk(st, *, tt):
    _, _, t = st.shape
    spec = pl.BlockSpec((None, PEER_NKEYS, tt), lambda i, h: (h, 0, i))
    shape = lambda dt: jax.ShapeDtypeStruct((PEER_HEADS, PEER_NKEYS, t), dt)
    return pl.pallas_call(
        _peer_topk_kernel,
        grid=(t // tt, PEER_HEADS),
        in_specs=[pl.BlockSpec((2, PEER_NKEYS, tt), lambda i, h: (h, 0, i))],
        out_specs=[spec, spec, spec, spec],
        out_shape=[shape(F32), shape(BF16), shape(F32), shape(BF16)],
        compiler_params=_params(("parallel", "parallel")),
        name="peer_topk",
    )(st)


def _gelu_exact(x):
    return 0.5 * x * (1.0 + lax.erf(x * (2.0 ** -0.5)))


def _peer_dense_kernel(xt_ref, u_ref, v_ref, e1_ref, rank_ref, e2_ref, cnt_ref, h_ref, o_ref, gate_sc, *,
                       na, halves):
    j = pl.program_id(1)

    def build_gate(chunk, gate_sc):
        sub = 16
        bshape = (sub, V7X_LANES)
        for a in range(na):
            row = chunk * na + a
            rkrows = [rank_ref[h, pl.ds(row, 1), :] for h in range(PEER_HEADS)]
            e1rows = [e1_ref[h, pl.ds(row, 1), :] for h in range(PEER_HEADS)]
            for l0 in range(0, gate_sc.shape[1], V7X_LANES):
                ls = slice(l0, l0 + V7X_LANES)
                rkb = [jnp.broadcast_to(rkrows[h][:, ls], bshape).astype(BF16) for h in range(PEER_HEADS)]
                e1b = [jnp.broadcast_to(e1rows[h][:, ls], bshape).astype(BF16) for h in range(PEER_HEADS)]
                for r in range(0, PEER_NKEYS, sub):
                    gate = None
                    for h in range(PEER_HEADS):
                        sel = rkb[h] < cnt_ref[h, r:r + sub, ls]
                        term = jnp.where(sel, e1b[h] * e2_ref[h, r:r + sub, ls], jnp.zeros(bshape, BF16))
                        gate = term if gate is None else gate + term
                    gate_sc[a * PEER_NKEYS + r:a * PEER_NKEYS + r + sub, ls] = gate

    @pl.when(j == 0)
    def _():
        o_ref[...] = h_ref[...]

    build_gate(j, gate_sc)
    rows = na * PEER_NKEYS // halves
    total = None
    for k in range(halves):
        sl = slice(k * rows, (k + 1) * rows)
        act = jnp.dot(u_ref[sl, :], xt_ref[...], preferred_element_type=F32)
        w = gate_sc[sl, :] * _gelu_exact(act).astype(BF16)
        part = _tn(w, v_ref[sl, :])
        total = part if total is None else total + part
    o_ref[...] += total


def _peer_dense(xnt, u_tab, v_tab, e1, rank1, e2, cnt, h, *, tt, ec):
    d, t = xnt.shape
    n_exp = u_tab.shape[0]
    na = ec // PEER_NKEYS
    once = pl.Buffered(1)
    kern = functools.partial(_peer_dense_kernel, na=na, halves=2)
    table = pl.BlockSpec((PEER_HEADS, PEER_NKEYS, tt), lambda i, j: (0, 0, i), pipeline_mode=once)
    return pl.pallas_call(
        kern,
        grid=(t // tt, n_exp // ec),
        in_specs=[pl.BlockSpec((d, tt), lambda i, j: (0, i), pipeline_mode=once),
                  pl.BlockSpec((ec, d), lambda i, j: (j, 0)),
                  pl.BlockSpec((ec, d), lambda i, j: (j, 0)),
                  table, table, table, table,
                  pl.BlockSpec((tt, d), lambda i, j: (i, 0), pipeline_mode=once)],
        out_specs=pl.BlockSpec((tt, d), lambda i, j: (i, 0)),
        out_shape=jax.ShapeDtypeStruct((t, d), F32),
        scratch_shapes=[pltpu.VMEM((ec, tt), BF16)],
        compiler_params=_params(("parallel", "arbitrary"), V7X_VMEM_BYTES - 4 * 1024 * 1024),
        name="peer_dense",
    )(xnt, u_tab, v_tab, e1, rank1, e2, cnt, h)


def _rotary_tables(pos, valid):
    half = RET_DK // 2
    inv = ROPE_BASE ** (-jnp.arange(half, dtype=F32) / half)
    ang = pos[:, None] * inv[None, :]
    cos, sin = jnp.cos(ang), jnp.sin(ang)
    c = jnp.concatenate([cos, cos], axis=-1)
    s = jnp.concatenate([-sin, sin], axis=-1)
    kscale = (RET_DK ** -0.5) * valid.astype(F32)[:, None]
    return jnp.stack([c, c * kscale]), jnp.stack([s, s * kscale])


def _tile(n, pref):
    t = min(n, pref)
    assert n % t == 0
    return t


def kernel(x, meta_tokens, norm_mix_g, w_in, b_forget, q_norm_g, k_norm_g, ret_norm_g, w_proj_fox,
           w_proj_ret, w_out, norm_ffn_g, peer_w_q, peer_keys_1, peer_keys_2, peer_u, peer_v):
    b, s, d = x.shape
    depth = w_in.shape[0]
    assert depth == 1, "only the real-token outputs of a single layer are computed"
    m = b * s
    xr = x.reshape(m, d)
    hm = jnp.concatenate([jnp.zeros((PAD, d), x.dtype), meta_tokens.astype(x.dtype)], axis=0)

    tm = _tile(m, 1024)
    tn = _tile(d, 1024)
    wt = jnp.swapaxes(w_in[0], 0, 1).astype(BF16)
    o_f = 3 * FOX_W
    o_r = o_f + FOX_HEADS
    groups = [(0, 2 * FOX_W), (o_r, 2 * RET_QK_W), (2 * FOX_W, FOX_W),
              (o_r + 2 * RET_QK_W, 2 * RET_V_W + 2 * d)]
    src_rows = jnp.asarray([start + t * tn for start, width in groups for t in range(width // tn)],
                           jnp.int32)
    npk = src_rows.shape[0] * tn
    wft = jnp.pad(wt[o_f:o_f + FOX_HEADS], ((0, V7X_LANES - FOX_HEADS), (0, 0)))
    bf = jnp.pad(b_forget[0], (0, V7X_LANES - FOX_HEADS)).reshape(1, V7X_LANES)
    gain = jnp.concatenate([jnp.tile(q_norm_g[0] * (FOX_HD ** -0.5 * LOG2_E), FOX_HEADS),
                            jnp.tile(k_norm_g[0], FOX_HEADS),
                            jnp.zeros((npk - 2 * FOX_W,), F32)]).reshape(1, npk)

    pos_real = jnp.arange(s, dtype=F32) + N_META
    rc, rs = _rotary_tables(pos_real, jnp.ones((s,), bool))
    pos_meta = jnp.arange(BLOCK, dtype=F32) - PAD
    rcm, rsm = _rotary_tables(pos_meta, jnp.arange(BLOCK) >= PAD)

    u = _rmsnorm_bf16(xr, norm_mix_g[0], _tile(m, 512))
    um = _rmsnorm_bf16(hm, norm_mix_g[0], BLOCK)
    z = _proj(u, wt, src_rows, gain, rc, rs, tm=_tile(s, 1024), tn=tn, d_model=d, first_valid_row=0)
    zm = _proj(um, wt, src_rows, gain, rcm, rsm, tm=BLOCK, tn=tn, d_model=d, first_valid_row=PAD)
    cm, _, ckxm = _forget_cumsum(um, wft, bf, jnp.zeros((1, V7X_LANES), F32), nb=1, tm=BLOCK,
                                 first_valid_row=PAD)
    _, cqx, ckx = _forget_cumsum(u, wft, bf, cm[BLOCK - 1:BLOCK], nb=b, tm=_tile(s, 512), first_valid_row=0)

    z3 = z.reshape(b, s, npk)
    ya = _fox_attention(z3, zm, cqx.reshape(b, s, V7X_LANES), ckx.reshape(b, s, FOX_W), ckxm,
                        tq=_tile(s, 512), heads=16)
    chunk = _tile(s, 256)
    yr = _retention(z3, zm, ret_norm_g[0], chunk=chunk, heads=8)

    merged = _merge(ya.reshape(m, FOX_W), yr.reshape(m, RET_V_W), w_proj_fox[0].astype(BF16),
                    w_proj_ret[0].astype(BF16), z, tm=tm, tn=_tile(d, 512), d_model=d)
    h1 = _outproj(merged, w_out[0].astype(BF16), xr, tm=tm, tn=tn)

    xn, xnt = _rmsnorm_bf16(h1, norm_ffn_g[0], _tile(m, 512), with_transpose=True)
    keys = jnp.stack([peer_keys_1[0], peer_keys_2[0]], axis=1).reshape(
        2 * PEER_HEADS, PEER_NKEYS, PEER_HALF).astype(BF16)
    st = _peer_scores(xn, peer_w_q[0].astype(BF16), keys, tm=tm, tn=_tile(PEER_HEADS * PEER_QDIM, 1024))
    e1, e2, rank1, cnt = _peer_topk(st, tt=_tile(m, 512))
    out = _peer_dense(xnt, peer_u[0].astype(BF16), peer_v[0].astype(BF16), e1, rank1, e2, cnt, h1,
                      tt=_tile(m, 512), ec=512)
    return out.reshape(b, s, d)
```

```python
import functools

import jax
import jax.numpy as jnp
from jax import lax
from jax.experimental import pallas as pl
from jax.experimental.pallas import tpu as pltpu

N_META = 16
BLOCK = 128
PAD = BLOCK - N_META
EPS = 1e-6
NEG = -1e30
LOG2_E = 1.4426950408889634
FOX_HEADS = 16
FOX_HD = 128
FOX_W = FOX_HEADS * FOX_HD
RET_HEADS = 16
RET_DK = 128
RET_DV = 256
RET_QK_W = RET_HEADS * RET_DK
RET_V_W = RET_HEADS * RET_DV
ROPE_BASE = 10000.0
PEER_HEADS = 8
PEER_NKEYS = 128
PEER_N = PEER_NKEYS * PEER_NKEYS
PEER_QDIM = 256
PEER_HALF = PEER_QDIM // 2
PEER_TOPK = 16

V7X_LANES = 128
V7X_VMEM_BYTES = 64 * 1024 * 1024
V7X_VMEM_BUDGET = 56 * 1024 * 1024

BF16 = jnp.bfloat16
F32 = jnp.float32

Z_FQ = 0
Z_FK = Z_FQ + FOX_W
Z_RQ = Z_FK + FOX_W
Z_RK = Z_RQ + RET_QK_W
Z_FV = Z_RK + RET_QK_W
Z_RV = Z_FV + FOX_W
Z_GR = Z_RV + RET_V_W
Z_GA = Z_GR + RET_V_W


def _params(sem, vmem=None):
    return pltpu.CompilerParams(dimension_semantics=sem, vmem_limit_bytes=vmem)


def _nt(a, b):
    return lax.dot_general(a, b, (((1,), (1,)), ((), ())), preferred_element_type=F32)


def _tn(a, b):
    return lax.dot_general(a, b, (((0,), (0,)), ((), ())), preferred_element_type=F32)


def _sigmoid(x):
    return 0.5 + 0.5 * jnp.tanh(0.5 * x)


def _rmsnorm_kernel(x_ref, g_ref, o_ref, *ot_ref):
    x = x_ref[...]
    ms = jnp.mean(x * x, axis=-1, keepdims=True)
    y = (x * lax.rsqrt(ms + EPS) * g_ref[...]).astype(o_ref.dtype)
    o_ref[...] = y
    if ot_ref:
        ot_ref[0][...] = y.T


def _rmsnorm_bf16(x, g, tm, with_transpose=False):
    m, d = x.shape
    out_specs = [pl.BlockSpec((tm, d), lambda i: (i, 0))]
    out_shape = [jax.ShapeDtypeStruct((m, d), BF16)]
    if with_transpose:
        out_specs.append(pl.BlockSpec((d, tm), lambda i: (0, i)))
        out_shape.append(jax.ShapeDtypeStruct((d, m), BF16))
    out = pl.pallas_call(
        _rmsnorm_kernel,
        grid=(m // tm,),
        in_specs=[pl.BlockSpec((tm, d), lambda i: (i, 0)),
                  pl.BlockSpec((1, d), lambda i: (0, 0))],
        out_specs=out_specs,
        out_shape=out_shape,
        compiler_params=_params(("parallel",), V7X_VMEM_BUDGET),
        name="rmsnorm_bf16",
    )(x, g.reshape(1, d))
    return out if with_transpose else out[0]


def _proj_kernel(src_ref, u_ref, w_ref, gain_ref, rc_ref, rs_ref, o_ref, *, classes, first_valid_row):
    del src_ref
    j = pl.program_id(1)
    acc = _nt(u_ref[...], w_ref[...])
    tm, tn = acc.shape
    groups = [slice(g * V7X_LANES, (g + 1) * V7X_LANES) for g in range(tn // V7X_LANES)]

    def emit(kind):
        if kind == "headnorm":
            for sl in groups:
                x = acc[:, sl]
                ms = jnp.mean(x * x, axis=-1, keepdims=True)
                o_ref[:, sl] = (x * lax.rsqrt(ms + EPS) * gain_ref[:, sl]).astype(o_ref.dtype)
        elif kind == "rotary":
            c = rc_ref[...]
            s = rs_ref[...]
            for sl in groups:
                x = acc[:, sl]
                o_ref[:, sl] = (x * c + pltpu.roll(x, V7X_LANES // 2, 1) * s).astype(o_ref.dtype)
        elif kind == "plain":
            o_ref[...] = acc.astype(o_ref.dtype)
        elif kind == "vmasked":
            if first_valid_row > 0:
                rows = lax.broadcasted_iota(jnp.int32, (tm, 1), 0) + pl.program_id(0) * tm
                o_ref[...] = jnp.where(rows >= first_valid_row, acc, 0.0).astype(o_ref.dtype)
            else:
                o_ref[...] = acc.astype(o_ref.dtype)
        elif kind == "silu":
            o_ref[...] = (acc * _sigmoid(acc)).astype(o_ref.dtype)
        elif kind == "sigmoid":
            o_ref[...] = _sigmoid(acc).astype(o_ref.dtype)
        else:
            raise ValueError(kind)

    for kind, lo, hi in classes:
        pl.when((j >= lo) & (j < hi))(functools.partial(emit, kind))


def _proj(u, wt, src_rows, gain, rot_c, rot_s, *, tm, tn, d_model, first_valid_row):
    m, d = u.shape
    npk = src_rows.shape[0] * tn
    n_row_tiles = rot_c.shape[1] // tm
    widths = (("headnorm", 2 * FOX_W), ("rotary", 2 * RET_QK_W), ("plain", FOX_W),
              ("vmasked", RET_V_W), ("silu", RET_V_W), ("sigmoid", 2 * d_model))
    classes, lo = [], 0
    for kind, w in widths:
        classes.append((kind, lo, lo + w // tn))
        lo += w // tn
    assert lo * tn == npk
    rk_lo = Z_RK // tn
    kern = functools.partial(_proj_kernel, classes=tuple(classes), first_valid_row=first_valid_row)
    rot_map = lambda i, j, src: ((j >= rk_lo).astype(jnp.int32), i % n_row_tiles, 0)
    grid_spec = pltpu.PrefetchScalarGridSpec(
        num_scalar_prefetch=1,
        grid=(m // tm, npk // tn),
        in_specs=[pl.BlockSpec((tm, d), lambda i, j, src: (i, 0)),
                  pl.BlockSpec((pl.Element(tn), pl.Element(d)),
                               lambda i, j, src: (pl.multiple_of(src[j], FOX_HEADS), 0)),
                  pl.BlockSpec((1, tn), lambda i, j, src: (0, j)),
                  pl.BlockSpec((None, tm, V7X_LANES), rot_map),
                  pl.BlockSpec((None, tm, V7X_LANES), rot_map)],
        out_specs=pl.BlockSpec((tm, tn), lambda i, j, src: (i, j)),
    )
    return pl.pallas_call(
        kern,
        grid_spec=grid_spec,
        out_shape=jax.ShapeDtypeStruct((m, npk), BF16),
        compiler_params=_params(("parallel", "arbitrary"), V7X_VMEM_BUDGET),
        name="input_proj",
    )(src_rows, u, wt, gain, rot_c, rot_s)


def _split3(x):
    hi = x.astype(BF16)
    r = x - hi.astype(F32)
    mid = r.astype(BF16)
    lo = (r - mid.astype(F32)).astype(BF16)
    return hi, mid, lo


def _forget_kernel(u_ref, w_ref, b_ref, cin_ref, selq_ref, selk_ref, o_ref, cqx_ref, ckx_ref, carry_ref, *,
                   first_valid_row):
    i = pl.program_id(1)

    @pl.when(i == 0)
    def _():
        carry_ref[...] = cin_ref[...]

    x = _nt(u_ref[...], w_ref[...]) + b_ref[...]
    tm = x.shape[0]
    logf = jnp.minimum(x, 0.0) - jnp.log1p(jnp.exp(-jnp.abs(x)))
    rows = lax.broadcasted_iota(jnp.int32, (tm, tm), 0)
    cols = lax.broadcasted_iota(jnp.int32, (tm, tm), 1)
    if first_valid_row > 0:
        r1 = lax.broadcasted_iota(jnp.int32, (tm, 1), 0) + i * tm
        logf = jnp.where(r1 >= first_valid_row, logf, 0.0)
    tri = jnp.where(rows >= cols, 1.0, 0.0).astype(BF16)
    hi, mid, lo = _split3(logf)
    cs = (jnp.dot(tri, hi, preferred_element_type=F32)
          + jnp.dot(tri, mid, preferred_element_type=F32)
          + jnp.dot(tri, lo, preferred_element_type=F32))
    c = cs + carry_ref[...]
    o_ref[...] = c
    carry_ref[...] = c[tm - 1:tm, :]
    chi, cmid, clo = _split3(c * LOG2_E)
    lane = lax.broadcasted_iota(jnp.int32, c.shape, 1)
    one = jnp.where(lane < FOX_HEADS, 1.0, 0.0).astype(BF16)
    pq = jnp.concatenate([chi, cmid, clo, one, one, one], axis=1)
    pk = jnp.concatenate([one, one, one, -chi, -cmid, -clo], axis=1)
    cqx_ref[...] = jnp.dot(pq, selq_ref[...], preferred_element_type=F32).astype(BF16)
    ckx_ref[...] = jnp.dot(pk, selk_ref[...], preferred_element_type=F32).astype(BF16)


FOX_EXT = 6


def _forget_selectors():
    import numpy as np
    selq = np.zeros((FOX_EXT * V7X_LANES, V7X_LANES), np.float32)
    selk = np.zeros((FOX_EXT * V7X_LANES, FOX_W), np.float32)
    for p in range(FOX_EXT):
        for h in range(FOX_HEADS):
            selq[p * V7X_LANES + h, FOX_EXT * h + p] = 1.0
            selk[p * V7X_LANES + h, FOX_HD * h + FOX_EXT * h + p] = 1.0
    return jnp.asarray(selq, BF16), jnp.asarray(selk, BF16)


def _forget_cumsum(u, wf, bf, carry_in, *, nb, tm, first_valid_row):
    m, d = u.shape
    nt = m // (nb * tm)
    selq, selk = _forget_selectors()
    kern = functools.partial(_forget_kernel, first_valid_row=first_valid_row)
    const = lambda b, i: (0, 0)
    row = lambda b, i: (b * nt + i, 0)
    return pl.pallas_call(
        kern,
        grid=(nb, nt),
        in_specs=[pl.BlockSpec((tm, d), row),
                  pl.BlockSpec((V7X_LANES, d), const),
                  pl.BlockSpec((1, V7X_LANES), const),
                  pl.BlockSpec((1, V7X_LANES), const),
                  pl.BlockSpec(selq.shape, const),
                  pl.BlockSpec(selk.shape, const)],
        out_specs=[pl.BlockSpec((tm, V7X_LANES), row),
                   pl.BlockSpec((tm, V7X_LANES), row),
                   pl.BlockSpec((tm, FOX_W), row)],
        out_shape=[jax.ShapeDtypeStruct((m, V7X_LANES), F32),
                   jax.ShapeDtypeStruct((m, V7X_LANES), BF16),
                   jax.ShapeDtypeStruct((m, FOX_W), BF16)],
        scratch_shapes=[pltpu.VMEM((1, V7X_LANES), F32)],
        compiler_params=_params(("parallel", "arbitrary"), V7X_VMEM_BUDGET),
        name="forget_cumsum",
    )(u, wf, bf, carry_in, selq, selk)


def _fox_kernel(qi_ref, kj_ref, kind_ref, q_ref, cqx_ref, km_ref, vm_ref, ckm_ref, k_ref, v_ref,
                ck_ref, o_ref, m_sc, acc_sc, qx_sc, *, heads):
    t = pl.program_id(2)
    kind = kind_ref[t]
    hd = FOX_HD
    lanes = [slice(g * hd, (g + 1) * hd) for g in range(heads)]

    def scores(g, k_r, ck_r):
        sl = lanes[g]
        k_ext = jnp.concatenate([k_r[:, sl], ck_r[:, sl]], axis=1)
        return _nt(k_ext, qx_sc[g])

    def pv(p, v):
        ones = jnp.ones((16, v.shape[0]), v.dtype)
        return jnp.dot(jnp.concatenate([v.T, ones], axis=0), p, preferred_element_type=F32)

    def update(g, s, v):
        m_prev = m_sc[g]
        m_new = jnp.maximum(m_prev, jnp.max(s, axis=0, keepdims=True))
        alpha = jnp.exp2(m_prev - m_new)
        p = jnp.exp2(s - m_new).astype(BF16)
        acc_sc[g] = alpha * acc_sc[g] + pv(p, v)
        m_sc[g] = m_new

    @pl.when(kind == 0)
    def _():
        for g, sl in enumerate(lanes):
            qx_sc[g] = jnp.concatenate([q_ref[:, sl], cqx_ref[...]], axis=1)
            s = scores(g, km_ref, ckm_ref)
            kpos = lax.broadcasted_iota(jnp.int32, s.shape, 0)
            s = jnp.where(kpos >= PAD, s, NEG)
            m = jnp.max(s, axis=0, keepdims=True)
            p = jnp.exp2(s - m).astype(BF16)
            m_sc[g] = m
            acc_sc[g] = pv(p, vm_ref[:, sl])

    @pl.when(kind == 1)
    def _():
        for g, sl in enumerate(lanes):
            update(g, scores(g, k_ref, ck_ref), v_ref[:, sl])

    @pl.when(kind == 2)
    def _():
        for g, sl in enumerate(lanes):
            s = scores(g, k_ref, ck_ref)
            krow = lax.broadcasted_iota(jnp.int32, s.shape, 0)
            qcol = lax.broadcasted_iota(jnp.int32, s.shape, 1)
            s = jnp.where(krow <= qcol, s, NEG)
            update(g, s, v_ref[:, sl])
            acc = acc_sc[g]
            o_ref[:, sl] = (acc[:hd, :] / acc[hd:hd + 1, :]).T.astype(o_ref.dtype)


def _fox_attention(z3, zm, cqx, ckx, ckxm, *, tq, heads):
    b, s, _ = z3.shape
    nq = s // tq
    qi, kj, kind = [], [], []
    for i in range(nq):
        qi += [i] * (i + 2)
        kj += [0] + list(range(i + 1))
        kind += [0] + [1] * i + [2]
    n_steps = len(qi)
    qi = jnp.asarray(qi, jnp.int32)
    kj = jnp.asarray(kj, jnp.int32)
    kind = jnp.asarray(kind, jnp.int32)
    w = heads * FOX_HD
    fq, fk, fv = Z_FQ // w, Z_FK // w, Z_FV // w
    grid_spec = pltpu.PrefetchScalarGridSpec(
        num_scalar_prefetch=3,
        grid=(b, FOX_HEADS // heads, n_steps),
        in_specs=[
            pl.BlockSpec((None, tq, w), lambda b_, h, t, qi, kj, kd: (b_, qi[t], fq + h)),
            pl.BlockSpec((None, tq, V7X_LANES), lambda b_, h, t, qi, kj, kd: (b_, qi[t], 0)),
            pl.BlockSpec((BLOCK, w), lambda b_, h, t, qi, kj, kd: (0, fk + h)),
            pl.BlockSpec((BLOCK, w), lambda b_, h, t, qi, kj, kd: (0, fv + h)),
            pl.BlockSpec((BLOCK, w), lambda b_, h, t, qi, kj, kd: (0, h)),
            pl.BlockSpec((None, tq, w), lambda b_, h, t, qi, kj, kd: (b_, kj[t], fk + h)),
            pl.BlockSpec((None, tq, w), lambda b_, h, t, qi, kj, kd: (b_, kj[t], fv + h)),
            pl.BlockSpec((None, tq, w), lambda b_, h, t, qi, kj, kd: (b_, kj[t], h)),
        ],
        out_specs=pl.BlockSpec((None, tq, w), lambda b_, h, t, qi, kj, kd: (b_, qi[t], h)),
        scratch_shapes=[pltpu.VMEM((heads, 1, tq), F32),
                        pltpu.VMEM((heads, FOX_HD + 16, tq), F32),
                        pltpu.VMEM((heads, tq, 2 * FOX_HD), BF16)],
    )
    return pl.pallas_call(
        functools.partial(_fox_kernel, heads=heads),
        grid_spec=grid_spec,
        out_shape=jax.ShapeDtypeStruct((b, s, FOX_W), BF16),
        compiler_params=_params(("parallel", "parallel", "arbitrary")),
        name="fox_attention",
    )(qi, kj, kind, z3, cqx, zm, zm, ckxm, z3, z3, ckx)


def _ret_kernel(q_ref, k_ref, v_ref, g_ref, km_ref, vm_ref, gn_ref, o_ref, state_sc, decay_sc,
                xi_sc, zeta_sc, cd_sc, *, heads):
    hg = pl.program_id(1)
    c = pl.program_id(2)
    chunk = q_ref.shape[0]
    dk, dv = RET_DK, RET_DV

    def log_gamma(head, width):
        hv = jnp.full((1, width), head, jnp.int32).astype(F32)
        return jnp.log1p(-jnp.exp2(-5.0 - hv))

    def rows(n, width):
        return lax.broadcasted_iota(jnp.int32, (n, width), 0).astype(F32)

    @pl.when(c == 0)
    def _():
        for g in range(heads):
            head = hg * heads + g
            diff = rows(chunk, chunk) - lax.broadcasted_iota(jnp.int32, (chunk, chunk), 1).astype(F32)
            decay_sc[g] = jnp.where(diff >= 0, jnp.exp(log_gamma(head, chunk) * jnp.maximum(diff, 0.0)), 0.0)
            xi_sc[g] = jnp.exp(log_gamma(head, dv) * (rows(chunk, dv) + 1.0))
            zeta_sc[g] = jnp.exp(log_gamma(head, dk) * (chunk - 1.0 - rows(chunk, dk)))
            cd_sc[g] = jnp.exp(log_gamma(head, dv) * float(chunk))
            zeta_m = jnp.exp(log_gamma(head, dk) * (BLOCK - 1.0 - rows(BLOCK, dk)))
            kz = (km_ref[:, g * dk:(g + 1) * dk].astype(F32) * zeta_m).astype(BF16)
            state_sc[g] = _tn(kz, vm_ref[:, g * dv:(g + 1) * dv])

    for g in range(heads):
        q = q_ref[:, g * dk:(g + 1) * dk]
        k = k_ref[:, g * dk:(g + 1) * dk]
        v = v_ref[:, g * dv:(g + 1) * dv]
        s = _nt(q, k) * decay_sc[g]
        intra = jnp.dot(s.astype(BF16), v, preferred_element_type=F32)
        state = state_sc[g]
        cross = jnp.dot(q, state.astype(BF16), preferred_element_type=F32) * xi_sc[g]
        o = intra + cross
        kz = (k.astype(F32) * zeta_sc[g]).astype(BF16)
        state_sc[g] = state * cd_sc[g] + _tn(kz, v)
        mu = jnp.mean(o, axis=-1, keepdims=True)
        d = o - mu
        var = jnp.mean(d * d, axis=-1, keepdims=True)
        y = d * lax.rsqrt(var + EPS) * gn_ref[:, g * dv:(g + 1) * dv]
        o_ref[:, g * dv:(g + 1) * dv] = (g_ref[:, g * dv:(g + 1) * dv].astype(F32) * y).astype(o_ref.dtype)


def _retention(z3, zm, ret_norm_g, *, chunk, heads):
    b, s, _ = z3.shape
    nc = s // chunk
    wk, wv = heads * RET_DK, heads * RET_DV
    rq, rk = Z_RQ // wk, Z_RK // wk
    rv, gr = Z_RV // wv, Z_GR // wv
    return pl.pallas_call(
        functools.partial(_ret_kernel, heads=heads),
        grid=(b, RET_HEADS // heads, nc),
        in_specs=[
            pl.BlockSpec((None, chunk, wk), lambda b_, h, c: (b_, c, rq + h)),
            pl.BlockSpec((None, chunk, wk), lambda b_, h, c: (b_, c, rk + h)),
            pl.BlockSpec((None, chunk, wv), lambda b_, h, c: (b_, c, rv + h)),
            pl.BlockSpec((None, chunk, wv), lambda b_, h, c: (b_, c, gr + h)),
            pl.BlockSpec((BLOCK, wk), lambda b_, h, c: (0, rk + h)),
            pl.BlockSpec((BLOCK, wv), lambda b_, h, c: (0, rv + h)),
            pl.BlockSpec((1, wv), lambda b_, h, c: (0, h)),
        ],
        out_specs=pl.BlockSpec((None, chunk, wv), lambda b_, h, c: (b_, c, h)),
        out_shape=jax.ShapeDtypeStruct((b, s, RET_V_W), BF16),
        scratch_shapes=[pltpu.VMEM((heads, RET_DK, RET_DV), F32),
                        pltpu.VMEM((heads, chunk, chunk), F32),
                        pltpu.VMEM((heads, chunk, RET_DV), F32),
                        pltpu.VMEM((heads, chunk, RET_DK), F32),
                        pltpu.VMEM((heads, 1, RET_DV), F32)],
        compiler_params=_params(("parallel", "parallel", "arbitrary")),
        name="retention",
    )(z3, z3, z3, z3, zm, zm, ret_norm_g.reshape(1, RET_V_W))


def _merge_kernel(ya_ref, yr_ref, wf_ref, wr_ref, ga_ref, gr_ref, o_ref):
    a = jnp.dot(ya_ref[...], wf_ref[...], preferred_element_type=F32)
    r = jnp.dot(yr_ref[...], wr_ref[...], preferred_element_type=F32)
    o_ref[...] = (ga_ref[...].astype(F32) * a + gr_ref[...].astype(F32) * r).astype(o_ref.dtype)


def _merge(ya, yr, wf, wr, z, *, tm, tn, d_model):
    m = ya.shape[0]
    ga0 = Z_GA // tn
    gr0 = (Z_GA + d_model) // tn
    return pl.pallas_call(
        _merge_kernel,
        grid=(m // tm, d_model // tn),
        in_specs=[pl.BlockSpec((tm, FOX_W), lambda i, j: (i, 0)),
                  pl.BlockSpec((tm, RET_V_W), lambda i, j: (i, 0)),
                  pl.BlockSpec((FOX_W, tn), lambda i, j: (0, j)),
                  pl.BlockSpec((RET_V_W, tn), lambda i, j: (0, j)),
                  pl.BlockSpec((tm, tn), lambda i, j: (i, ga0 + j)),
                  pl.BlockSpec((tm, tn), lambda i, j: (i, gr0 + j))],
        out_specs=pl.BlockSpec((tm, tn), lambda i, j: (i, j)),
        out_shape=jax.ShapeDtypeStruct((m, d_model), BF16),
        compiler_params=_params(("parallel", "arbitrary"), V7X_VMEM_BUDGET),
        name="merge_proj",
    )(ya, yr, wf, wr, z, z)


def _outproj_kernel(a_ref, w_ref, x_ref, o_ref):
    o_ref[...] = x_ref[...] + jnp.dot(a_ref[...], w_ref[...], preferred_element_type=F32)


def _outproj(a, w, x, *, tm, tn):
    m, k = a.shape
    n = w.shape[1]
    return pl.pallas_call(
        _outproj_kernel,
        grid=(m // tm, n // tn),
        in_specs=[pl.BlockSpec((tm, k), lambda i, j: (i, 0)),
                  pl.BlockSpec((k, tn), lambda i, j: (0, j)),
                  pl.BlockSpec((tm, tn), lambda i, j: (i, j))],
        out_specs=pl.BlockSpec((tm, tn), lambda i, j: (i, j)),
        out_shape=jax.ShapeDtypeStruct((m, n), F32),
        compiler_params=_params(("parallel", "arbitrary"), V7X_VMEM_BUDGET),
        name="out_proj",
    )(a, w, x)


def _peer_score_kernel(x_ref, w_ref, keys_ref, o_ref):
    acc = jnp.dot(x_ref[...], w_ref[...], preferred_element_type=F32)
    for g in range(acc.shape[1] // PEER_HALF):
        qg = acc[:, g * PEER_HALF:(g + 1) * PEER_HALF].astype(BF16)
        o_ref[g] = _nt(keys_ref[g], qg)


def _peer_scores(xn, wq, keys, *, tm, tn):
    m, d = xn.shape
    n = wq.shape[1]
    ng = tn // PEER_HALF
    return pl.pallas_call(
        _peer_score_kernel,
        grid=(m // tm, n // tn),
        in_specs=[pl.BlockSpec((tm, d), lambda i, j: (i, 0)),
                  pl.BlockSpec((d, tn), lambda i, j: (0, j)),
                  pl.BlockSpec((ng, PEER_NKEYS, PEER_HALF), lambda i, j: (j, 0, 0))],
        out_specs=pl.BlockSpec((ng, PEER_NKEYS, tm), lambda i, j: (j, 0, i)),
        out_shape=jax.ShapeDtypeStruct((n // PEER_HALF, PEER_NKEYS, m), F32),
        compiler_params=_params(("parallel", "arbitrary"), V7X_VMEM_BUDGET),
        name="peer_scores",
    )(xn, wq, keys)


def _top_values(x, n, stacked_rows):
    rows, width = x.shape
    idx = lax.broadcasted_iota(jnp.int32, (rows, width), 0).astype(F32)
    srow = lax.broadcasted_iota(jnp.int32, (stacked_rows, width), 0)
    stacked = jnp.zeros((stacked_rows, width), F32)
    rank = jnp.full((rows, width), float(n), F32)
    vals = []
    for t in range(n):
        cur = jnp.max(x, axis=0, keepdims=True)
        vals.append(cur)
        stacked = jnp.where(srow == t, cur, stacked)
        first = jnp.min(jnp.where(x == cur, idx, float(rows)), axis=0, keepdims=True)
        hit = idx == first
        rank = jnp.where(hit, float(t), rank)
        x = jnp.where(hit, -jnp.inf, x)
    return stacked, vals, rank


def _peer_topk_kernel(s_ref, e1_ref, e2_ref, rank_ref, cnt_ref):
    s1 = s_ref[0]
    s2 = s_ref[1]
    v1, v1l, rank1 = _top_values(s1, PEER_TOPK, PEER_TOPK)
    v2, v2l, _ = _top_values(s2, PEER_TOPK, PEER_TOPK)
    sub = lax.broadcasted_iota(jnp.int32, (8, s1.shape[1]), 0)
    pieces = [v1l[0] + v2]
    for a in range(1, 8):
        nb = PEER_TOPK // (a + 1)
        pieces.append(jnp.where(sub < nb, v1l[a] + v2[0:8], -jnp.inf))
    pieces.append(v1[8:16] + v2l[0])
    cand = jnp.concatenate(pieces, axis=0)
    _, top, _ = _top_values(cand, PEER_TOPK, 8)
    m = top[0]
    z = jnp.zeros_like(m)
    for t in range(PEER_TOPK):
        z = z + jnp.exp(top[t] - m)
    tau = top[PEER_TOPK - 1]
    cnt = jnp.zeros_like(s2)
    for a in range(PEER_TOPK):
        cnt = cnt + jnp.where(v1l[a] + s2 >= tau, 1.0, 0.0)
    rank_ref[...] = rank1
    cnt_ref[...] = cnt.astype(cnt_ref.dtype)
    e1_ref[...] = jnp.exp(s1 - v1l[0])
    e2_ref[...] = (jnp.exp(s2 - v2l[0]) / z).astype(e2_ref.dtype)


def _peer_topk(st, *, tt):
    _, _, t = st.shape
    spec = pl.BlockSpec((None, PEER_NKEYS, tt), lambda i, h: (h, 0, i))
    shape = lambda dt: jax.ShapeDtypeStruct((PEER_HEADS, PEER_NKEYS, t), dt)
    return pl.pallas_call(
        _peer_topk_kernel,
        grid=(t // tt, PEER_HEADS),
        in_specs=[pl.BlockSpec((2, PEER_NKEYS, tt), lambda i, h: (h, 0, i))],
        out_specs=[spec, spec, spec, spec],
        out_shape=[shape(F32), shape(BF16), shape(F32), shape(BF16)],
        compiler_params=_params(("parallel", "parallel")),
        name="peer_topk",
    )(st)


def _gelu_exact(x):
    return 0.5 * x * (1.0 + lax.erf(x * (2.0 ** -0.5)))


def _peer_dense_kernel(xt_ref, u_ref, v_ref, e1_ref, rank_ref, e2_ref, cnt_ref, h_ref, o_ref, gate_sc, *,
                       na, halves):
    j = pl.program_id(1)

    def build_gate(chunk, gate_sc):
        sub = 16
        bshape = (sub, V7X_LANES)
        for a in range(na):
            row = chunk * na + a
            rkrows = [rank_ref[h, pl.ds(row, 1), :] for h in range(PEER_HEADS)]
            e1rows = [e1_ref[h, pl.ds(row, 1), :] for h in range(PEER_HEADS)]
            for l0 in range(0, gate_sc.shape[1], V7X_LANES):
                ls = slice(l0, l0 + V7X_LANES)
                rkb = [jnp.broadcast_to(rkrows[h][:, ls], bshape).astype(BF16) for h in range(PEER_HEADS)]
                e1b = [jnp.broadcast_to(e1rows[h][:, ls], bshape).astype(BF16) for h in range(PEER_HEADS)]
                for r in range(0, PEER_NKEYS, sub):
                    gate = None
                    for h in range(PEER_HEADS):
                        sel = rkb[h] < cnt_ref[h, r:r + sub, ls]
                        term = jnp.where(sel, e1b[h] * e2_ref[h, r:r + sub, ls], jnp.zeros(bshape, BF16))
                        gate = term if gate is None else gate + term
                    gate_sc[a * PEER_NKEYS + r:a * PEER_NKEYS + r + sub, ls] = gate

    @pl.when(j == 0)
    def _():
        rows = o_ref.shape[0]
        pltpu.sync_copy(h_ref.at[pl.ds(pl.program_id(0) * rows, rows), :], o_ref)

    build_gate(j, gate_sc)
    rows = na * PEER_NKEYS // halves
    total = None
    for k in range(halves):
        sl = slice(k * rows, (k + 1) * rows)
        act = jnp.dot(u_ref[sl, :], xt_ref[...], preferred_element_type=F32)
        w = gate_sc[sl, :] * _gelu_exact(act).astype(BF16)
        part = _tn(w, v_ref[sl, :])
        total = part if total is None else total + part
    o_ref[...] += total


def _peer_dense(xnt, u_tab, v_tab, e1, rank1, e2, cnt, h, *, tt, ec):
    d, t = xnt.shape
    n_exp = u_tab.shape[0]
    na = ec // PEER_NKEYS
    once = pl.Buffered(1)
    kern = functools.partial(_peer_dense_kernel, na=na, halves=2)
    table = pl.BlockSpec((PEER_HEADS, PEER_NKEYS, tt), lambda i, j: (0, 0, i), pipeline_mode=once)
    return pl.pallas_call(
        kern,
        grid=(t // tt, n_exp // ec),
        in_specs=[pl.BlockSpec((d, tt), lambda i, j: (0, i), pipeline_mode=once),
                  pl.BlockSpec((ec, d), lambda i, j: (j, 0)),
                  pl.BlockSpec((ec, d), lambda i, j: (j, 0)),
                  table, table, table, table,
                  pl.BlockSpec(memory_space=pl.ANY)],
        out_specs=pl.BlockSpec((tt, d), lambda i, j: (i, 0), pipeline_mode=once),
        out_shape=jax.ShapeDtypeStruct((t, d), F32),
        scratch_shapes=[pltpu.VMEM((ec, tt), BF16)],
        compiler_params=_params(("parallel", "arbitrary"), V7X_VMEM_BYTES - 4 * 1024 * 1024),
        name="peer_dense",
    )(xnt, u_tab, v_tab, e1, rank1, e2, cnt, h)


def _rotary_tables(pos, valid):
    half = RET_DK // 2
    inv = ROPE_BASE ** (-jnp.arange(half, dtype=F32) / half)
    ang = pos[:, None] * inv[None, :]
    cos, sin = jnp.cos(ang), jnp.sin(ang)
    c = jnp.concatenate([cos, cos], axis=-1)
    s = jnp.concatenate([-sin, sin], axis=-1)
    kscale = (RET_DK ** -0.5) * valid.astype(F32)[:, None]
    return jnp.stack([c, c * kscale]), jnp.stack([s, s * kscale])


def _tile(n, pref):
    t = min(n, pref)
    assert n % t == 0
    return t


def kernel(x, meta_tokens, norm_mix_g, w_in, b_forget, q_norm_g, k_norm_g, ret_norm_g, w_proj_fox,
           w_proj_ret, w_out, norm_ffn_g, peer_w_q, peer_keys_1, peer_keys_2, peer_u, peer_v):
    b, s, d = x.shape
    depth = w_in.shape[0]
    assert depth == 1, "only the real-token outputs of a single layer are computed"
    m = b * s
    xr = x.reshape(m, d)
    hm = jnp.concatenate([jnp.zeros((PAD, d), x.dtype), meta_tokens.astype(x.dtype)], axis=0)

    tm = _tile(m, 1024)
    tn = _tile(d, 1024)
    wt = jnp.swapaxes(w_in[0], 0, 1).astype(BF16)
    o_f = 3 * FOX_W
    o_r = o_f + FOX_HEADS
    groups = [(0, 2 * FOX_W), (o_r, 2 * RET_QK_W), (2 * FOX_W, FOX_W),
              (o_r + 2 * RET_QK_W, 2 * RET_V_W + 2 * d)]
    src_rows = jnp.asarray([start + t * tn for start, width in groups for t in range(width // tn)],
                           jnp.int32)
    npk = src_rows.shape[0] * tn
    wft = jnp.pad(wt[o_f:o_f + FOX_HEADS], ((0, V7X_LANES - FOX_HEADS), (0, 0)))
    bf = jnp.pad(b_forget[0], (0, V7X_LANES - FOX_HEADS)).reshape(1, V7X_LANES)
    gain = jnp.concatenate([jnp.tile(q_norm_g[0] * (FOX_HD ** -0.5 * LOG2_E), FOX_HEADS),
                            jnp.tile(k_norm_g[0], FOX_HEADS),
                            jnp.zeros((npk - 2 * FOX_W,), F32)]).reshape(1, npk)

    pos_real = jnp.arange(s, dtype=F32) + N_META
    rc, rs = _rotary_tables(pos_real, jnp.ones((s,), bool))
    pos_meta = jnp.arange(BLOCK, dtype=F32) - PAD
    rcm, rsm = _rotary_tables(pos_meta, jnp.arange(BLOCK) >= PAD)

    u = _rmsnorm_bf16(xr, norm_mix_g[0], _tile(m, 512))
    um = _rmsnorm_bf16(hm, norm_mix_g[0], BLOCK)
    z = _proj(u, wt, src_rows, gain, rc, rs, tm=_tile(s, 1024), tn=tn, d_model=d, first_valid_row=0)
    zm = _proj(um, wt, src_rows, gain, rcm, rsm, tm=BLOCK, tn=tn, d_model=d, first_valid_row=PAD)
    cm, _, ckxm = _forget_cumsum(um, wft, bf, jnp.zeros((1, V7X_LANES), F32), nb=1, tm=BLOCK,
                                 first_valid_row=PAD)
    _, cqx, ckx = _forget_cumsum(u, wft, bf, cm[BLOCK - 1:BLOCK], nb=b, tm=_tile(s, 512), first_valid_row=0)

    z3 = z.reshape(b, s, npk)
    ya = _fox_attention(z3, zm, cqx.reshape(b, s, V7X_LANES), ckx.reshape(b, s, FOX_W), ckxm,
                        tq=_tile(s, 512), heads=16)
    chunk = _tile(s, 256)
    yr = _retention(z3, zm, ret_norm_g[0], chunk=chunk, heads=8)

    merged = _merge(ya.reshape(m, FOX_W), yr.reshape(m, RET_V_W), w_proj_fox[0].astype(BF16),
                    w_proj_ret[0].astype(BF16), z, tm=tm, tn=_tile(d, 512), d_model=d)
    h1 = _outproj(merged, w_out[0].astype(BF16), xr, tm=tm, tn=tn)

    xn, xnt = _rmsnorm_bf16(h1, norm_ffn_g[0], _tile(m, 512), with_transpose=True)
    keys = jnp.stack([peer_keys_1[0], peer_keys_2[0]], axis=1).reshape(
        2 * PEER_HEADS, PEER_NKEYS, PEER_HALF).astype(BF16)
    st = _peer_scores(xn, peer_w_q[0].astype(BF16), keys, tm=tm, tn=_tile(PEER_HEADS * PEER_QDIM, 1024))
    e1, e2, rank1, cnt = _peer_topk(st, tt=_tile(m, 512))
    out = _peer_dense(xnt, peer_u[0].astype(BF16), peer_v[0].astype(BF16), e1, rank1, e2, cnt, h1,
                      tt=_tile(m, 512), ec=1024)
    return out.reshape(b, s, d)
```

```python
import functools

import jax
import jax.numpy as jnp
from jax import lax
from jax.experimental import pallas as pl
from jax.experimental.pallas import tpu as pltpu

N_META = 16
BLOCK = 128
PAD = BLOCK - N_META
EPS = 1e-6
NEG = -1e30
LOG2_E = 1.4426950408889634
FOX_HEADS = 16
FOX_HD = 128
FOX_W = FOX_HEADS * FOX_HD
RET_HEADS = 16
RET_DK = 128
RET_DV = 256
RET_QK_W = RET_HEADS * RET_DK
RET_V_W = RET_HEADS * RET_DV
ROPE_BASE = 10000.0
PEER_HEADS = 8
PEER_NKEYS = 128
PEER_N = PEER_NKEYS * PEER_NKEYS
PEER_QDIM = 256
PEER_HALF = PEER_QDIM // 2
PEER_TOPK = 16

V7X_LANES = 128
V7X_VMEM_BYTES = 64 * 1024 * 1024
V7X_VMEM_BUDGET = 56 * 1024 * 1024

BF16 = jnp.bfloat16
F32 = jnp.float32

Z_FQ = 0
Z_FK = Z_FQ + FOX_W
Z_RQ = Z_FK + FOX_W
Z_RK = Z_RQ + RET_QK_W
Z_FV = Z_RK + RET_QK_W
Z_RV = Z_FV + FOX_W
Z_GR = Z_RV + RET_V_W
Z_GA = Z_GR + RET_V_W


def _params(sem, vmem=None):
    return pltpu.CompilerParams(dimension_semantics=sem, vmem_limit_bytes=vmem)


def _nt(a, b):
    return lax.dot_general(a, b, (((1,), (1,)), ((), ())), preferred_element_type=F32)


def _tn(a, b):
    return lax.dot_general(a, b, (((0,), (0,)), ((), ())), preferred_element_type=F32)


def _sigmoid(x):
    return 0.5 + 0.5 * jnp.tanh(0.5 * x)


def _rmsnorm_kernel(x_ref, g_ref, o_ref, *ot_ref):
    x = x_ref[...]
    ms = jnp.mean(x * x, axis=-1, keepdims=True)
    y = (x * lax.rsqrt(ms + EPS) * g_ref[...]).astype(o_ref.dtype)
    o_ref[...] = y
    if ot_ref:
        ot_ref[0][...] = y.T


def _rmsnorm_bf16(x, g, tm, with_transpose=False):
    m, d = x.shape
    out_specs = [pl.BlockSpec((tm, d), lambda i: (i, 0))]
    out_shape = [jax.ShapeDtypeStruct((m, d), BF16)]
    if with_transpose:
        out_specs.append(pl.BlockSpec((d, tm), lambda i: (0, i)))
        out_shape.append(jax.ShapeDtypeStruct((d, m), BF16))
    out = pl.pallas_call(
        _rmsnorm_kernel,
        grid=(m // tm,),
        in_specs=[pl.BlockSpec((tm, d), lambda i: (i, 0)),
                  pl.BlockSpec((1, d), lambda i: (0, 0))],
        out_specs=out_specs,
        out_shape=out_shape,
        compiler_params=_params(("parallel",), V7X_VMEM_BUDGET),
        name="rmsnorm_bf16",
    )(x, g.reshape(1, d))
    return out if with_transpose else out[0]


def _proj_kernel(src_ref, u_ref, w_ref, gain_ref, rc_ref, rs_ref, o_ref, *, classes, first_valid_row):
    del src_ref
    j = pl.program_id(1)
    acc = _nt(u_ref[...], w_ref[...])
    tm, tn = acc.shape
    groups = [slice(g * V7X_LANES, (g + 1) * V7X_LANES) for g in range(tn // V7X_LANES)]

    def emit(kind):
        if kind == "headnorm":
            for sl in groups:
                x = acc[:, sl]
                ms = jnp.mean(x * x, axis=-1, keepdims=True)
                o_ref[:, sl] = (x * lax.rsqrt(ms + EPS) * gain_ref[:, sl]).astype(o_ref.dtype)
        elif kind == "rotary":
            c = rc_ref[...]
            s = rs_ref[...]
            for sl in groups:
                x = acc[:, sl]
                o_ref[:, sl] = (x * c + pltpu.roll(x, V7X_LANES // 2, 1) * s).astype(o_ref.dtype)
        elif kind == "plain":
            o_ref[...] = acc.astype(o_ref.dtype)
        elif kind == "vmasked":
            if first_valid_row > 0:
                rows = lax.broadcasted_iota(jnp.int32, (tm, 1), 0) + pl.program_id(0) * tm
                o_ref[...] = jnp.where(rows >= first_valid_row, acc, 0.0).astype(o_ref.dtype)
            else:
                o_ref[...] = acc.astype(o_ref.dtype)
        elif kind == "silu":
            o_ref[...] = (acc * _sigmoid(acc)).astype(o_ref.dtype)
        elif kind == "sigmoid":
            o_ref[...] = _sigmoid(acc).astype(o_ref.dtype)
        else:
            raise ValueError(kind)

    for kind, lo, hi in classes:
        pl.when((j >= lo) & (j < hi))(functools.partial(emit, kind))


def _proj(u, wt, src_rows, gain, rot_c, rot_s, *, tm, tn, d_model, first_valid_row):
    m, d = u.shape
    npk = src_rows.shape[0] * tn
    n_row_tiles = rot_c.shape[1] // tm
    widths = (("headnorm", 2 * FOX_W), ("rotary", 2 * RET_QK_W), ("plain", FOX_W),
              ("vmasked", RET_V_W), ("silu", RET_V_W), ("sigmoid", 2 * d_model))
    classes, lo = [], 0
    for kind, w in widths:
        classes.append((kind, lo, lo + w // tn))
        lo += w // tn
    assert lo * tn == npk
    rk_lo = Z_RK // tn
    kern = functools.partial(_proj_kernel, classes=tuple(classes), first_valid_row=first_valid_row)
    rot_map = lambda i, j, src: ((j >= rk_lo).astype(jnp.int32), i % n_row_tiles, 0)
    grid_spec = pltpu.PrefetchScalarGridSpec(
        num_scalar_prefetch=1,
        grid=(m // tm, npk // tn),
        in_specs=[pl.BlockSpec((tm, d), lambda i, j, src: (i, 0)),
                  pl.BlockSpec((pl.Element(tn), pl.Element(d)),
                               lambda i, j, src: (pl.multiple_of(src[j], FOX_HEADS), 0)),
                  pl.BlockSpec((1, tn), lambda i, j, src: (0, j)),
                  pl.BlockSpec((None, tm, V7X_LANES), rot_map),
                  pl.BlockSpec((None, tm, V7X_LANES), rot_map)],
        out_specs=pl.BlockSpec((tm, tn), lambda i, j, src: (i, j)),
    )
    return pl.pallas_call(
        kern,
        grid_spec=grid_spec,
        out_shape=jax.ShapeDtypeStruct((m, npk), BF16),
        compiler_params=_params(("parallel", "arbitrary"), V7X_VMEM_BUDGET),
        name="input_proj",
    )(src_rows, u, wt, gain, rot_c, rot_s)


def _split3(x):
    hi = x.astype(BF16)
    r = x - hi.astype(F32)
    mid = r.astype(BF16)
    lo = (r - mid.astype(F32)).astype(BF16)
    return hi, mid, lo


def _forget_kernel(h_ref, g_ref, w_ref, b_ref, cin_ref, selq_ref, selk_ref, u_ref, o_ref, cqx_ref, ckx_ref,
                   carry_ref, *, first_valid_row):
    i = pl.program_id(1)

    @pl.when(i == 0)
    def _():
        carry_ref[...] = cin_ref[...]

    h = h_ref[...]
    ms = jnp.mean(h * h, axis=-1, keepdims=True)
    u = (h * lax.rsqrt(ms + EPS) * g_ref[...]).astype(u_ref.dtype)
    u_ref[...] = u
    x = _nt(u, w_ref[...]) + b_ref[...]
    tm = x.shape[0]
    logf = jnp.minimum(x, 0.0) - jnp.log1p(jnp.exp(-jnp.abs(x)))
    rows = lax.broadcasted_iota(jnp.int32, (tm, tm), 0)
    cols = lax.broadcasted_iota(jnp.int32, (tm, tm), 1)
    if first_valid_row > 0:
        r1 = lax.broadcasted_iota(jnp.int32, (tm, 1), 0) + i * tm
        logf = jnp.where(r1 >= first_valid_row, logf, 0.0)
    tri = jnp.where(rows >= cols, 1.0, 0.0).astype(BF16)
    hi, mid, lo = _split3(logf)
    cs = (jnp.dot(tri, hi, preferred_element_type=F32)
          + jnp.dot(tri, mid, preferred_element_type=F32)
          + jnp.dot(tri, lo, preferred_element_type=F32))
    c = cs + carry_ref[...]
    o_ref[...] = c
    carry_ref[...] = c[tm - 1:tm, :]
    chi, cmid, clo = _split3(c * LOG2_E)
    lane = lax.broadcasted_iota(jnp.int32, c.shape, 1)
    one = jnp.where(lane < FOX_HEADS, 1.0, 0.0).astype(BF16)
    pq = jnp.concatenate([chi, cmid, clo, one, one, one], axis=1)
    pk = jnp.concatenate([one, one, one, -chi, -cmid, -clo], axis=1)
    cqx_ref[...] = jnp.dot(pq, selq_ref[...], preferred_element_type=F32).astype(BF16)
    ckx_ref[...] = jnp.dot(pk, selk_ref[...], preferred_element_type=F32).astype(BF16)


FOX_EXT = 6


def _forget_selectors():
    import numpy as np
    selq = np.zeros((FOX_EXT * V7X_LANES, V7X_LANES), np.float32)
    selk = np.zeros((FOX_EXT * V7X_LANES, FOX_W), np.float32)
    for p in range(FOX_EXT):
        for h in range(FOX_HEADS):
            selq[p * V7X_LANES + h, FOX_EXT * h + p] = 1.0
            selk[p * V7X_LANES + h, FOX_HD * h + FOX_EXT * h + p] = 1.0
    return jnp.asarray(selq, BF16), jnp.asarray(selk, BF16)


def _forget_cumsum(h, g, wf, bf, carry_in, *, nb, tm, first_valid_row):
    m, d = h.shape
    nt = m // (nb * tm)
    selq, selk = _forget_selectors()
    kern = functools.partial(_forget_kernel, first_valid_row=first_valid_row)
    const = lambda b, i: (0, 0)
    row = lambda b, i: (b * nt + i, 0)
    return pl.pallas_call(
        kern,
        grid=(nb, nt),
        in_specs=[pl.BlockSpec((tm, d), row),
                  pl.BlockSpec((1, d), const),
                  pl.BlockSpec((V7X_LANES, d), const),
                  pl.BlockSpec((1, V7X_LANES), const),
                  pl.BlockSpec((1, V7X_LANES), const),
                  pl.BlockSpec(selq.shape, const),
                  pl.BlockSpec(selk.shape, const)],
        out_specs=[pl.BlockSpec((tm, d), row),
                   pl.BlockSpec((tm, V7X_LANES), row),
                   pl.BlockSpec((tm, V7X_LANES), row),
                   pl.BlockSpec((tm, FOX_W), row)],
        out_shape=[jax.ShapeDtypeStruct((m, d), BF16),
                   jax.ShapeDtypeStruct((m, V7X_LANES), F32),
                   jax.ShapeDtypeStruct((m, V7X_LANES), BF16),
                   jax.ShapeDtypeStruct((m, FOX_W), BF16)],
        scratch_shapes=[pltpu.VMEM((1, V7X_LANES), F32)],
        compiler_params=_params(("parallel", "arbitrary"), V7X_VMEM_BUDGET),
        name="norm_forget_cumsum",
    )(h, g.reshape(1, d), wf, bf, carry_in, selq, selk)


def _fox_kernel(qi_ref, kj_ref, kind_ref, q_ref, cqx_ref, km_ref, vm_ref, ckm_ref, k_ref, v_ref,
                ck_ref, o_ref, m_sc, acc_sc, qx_sc, *, heads):
    t = pl.program_id(2)
    kind = kind_ref[t]
    hd = FOX_HD
    lanes = [slice(g * hd, (g + 1) * hd) for g in range(heads)]

    def scores(g, k_r, ck_r):
        sl = lanes[g]
        k_ext = jnp.concatenate([k_r[:, sl], ck_r[:, sl]], axis=1)
        return _nt(k_ext, qx_sc[g])

    def pv(p, v):
        ones = jnp.ones((16, v.shape[0]), v.dtype)
        return jnp.dot(jnp.concatenate([v.T, ones], axis=0), p, preferred_element_type=F32)

    def update(g, s, v):
        m_prev = m_sc[g]
        m_new = jnp.maximum(m_prev, jnp.max(s, axis=0, keepdims=True))
        alpha = jnp.exp2(m_prev - m_new)
        p = jnp.exp2(s - m_new).astype(BF16)
        acc_sc[g] = alpha * acc_sc[g] + pv(p, v)
        m_sc[g] = m_new

    @pl.when(kind == 0)
    def _():
        for g, sl in enumerate(lanes):
            qx_sc[g] = jnp.concatenate([q_ref[:, sl], cqx_ref[...]], axis=1)
            s = scores(g, km_ref, ckm_ref)
            kpos = lax.broadcasted_iota(jnp.int32, s.shape, 0)
            s = jnp.where(kpos >= PAD, s, NEG)
            m = jnp.max(s, axis=0, keepdims=True)
            p = jnp.exp2(s - m).astype(BF16)
            m_sc[g] = m
            acc_sc[g] = pv(p, vm_ref[:, sl])

    @pl.when(kind == 1)
    def _():
        for g, sl in enumerate(lanes):
            update(g, scores(g, k_ref, ck_ref), v_ref[:, sl])

    @pl.when(kind == 2)
    def _():
        for g, sl in enumerate(lanes):
            s = scores(g, k_ref, ck_ref)
            krow = lax.broadcasted_iota(jnp.int32, s.shape, 0)
            qcol = lax.broadcasted_iota(jnp.int32, s.shape, 1)
            s = jnp.where(krow <= qcol, s, NEG)
            update(g, s, v_ref[:, sl])
            acc = acc_sc[g]
            o_ref[:, sl] = (acc[:hd, :] / acc[hd:hd + 1, :]).T.astype(o_ref.dtype)


def _fox_attention(z3, zm, cqx, ckx, ckxm, *, tq, heads):
    b, s, _ = z3.shape
    nq = s // tq
    qi, kj, kind = [], [], []
    for i in range(nq):
        qi += [i] * (i + 2)
        kj += [0] + list(range(i + 1))
        kind += [0] + [1] * i + [2]
    n_steps = len(qi)
    qi = jnp.asarray(qi, jnp.int32)
    kj = jnp.asarray(kj, jnp.int32)
    kind = jnp.asarray(kind, jnp.int32)
    w = heads * FOX_HD
    fq, fk, fv = Z_FQ // w, Z_FK // w, Z_FV // w
    grid_spec = pltpu.PrefetchScalarGridSpec(
        num_scalar_prefetch=3,
        grid=(b, FOX_HEADS // heads, n_steps),
        in_specs=[
            pl.BlockSpec((None, tq, w), lambda b_, h, t, qi, kj, kd: (b_, qi[t], fq + h)),
            pl.BlockSpec((None, tq, V7X_LANES), lambda b_, h, t, qi, kj, kd: (b_, qi[t], 0)),
            pl.BlockSpec((BLOCK, w), lambda b_, h, t, qi, kj, kd: (0, fk + h)),
            pl.BlockSpec((BLOCK, w), lambda b_, h, t, qi, kj, kd: (0, fv + h)),
            pl.BlockSpec((BLOCK, w), lambda b_, h, t, qi, kj, kd: (0, h)),
            pl.BlockSpec((None, tq, w), lambda b_, h, t, qi, kj, kd: (b_, kj[t], fk + h)),
            pl.BlockSpec((None, tq, w), lambda b_, h, t, qi, kj, kd: (b_, kj[t], fv + h)),
            pl.BlockSpec((None, tq, w), lambda b_, h, t, qi, kj, kd: (b_, kj[t], h)),
        ],
        out_specs=pl.BlockSpec((None, tq, w), lambda b_, h, t, qi, kj, kd: (b_, qi[t], h)),
        scratch_shapes=[pltpu.VMEM((heads, 1, tq), F32),
                        pltpu.VMEM((heads, FOX_HD + 16, tq), F32),
                        pltpu.VMEM((heads, tq, 2 * FOX_HD), BF16)],
    )
    return pl.pallas_call(
        functools.partial(_fox_kernel, heads=heads),
        grid_spec=grid_spec,
        out_shape=jax.ShapeDtypeStruct((b, s, FOX_W), BF16),
        compiler_params=_params(("parallel", "parallel", "arbitrary")),
        name="fox_attention",
    )(qi, kj, kind, z3, cqx, zm, zm, ckxm, z3, z3, ckx)


def _ret_kernel(q_ref, k_ref, v_ref, g_ref, km_ref, vm_ref, gn_ref, o_ref, state_sc, decay_sc,
                xi_sc, zeta_sc, cd_sc, *, heads):
    hg = pl.program_id(1)
    c = pl.program_id(2)
    chunk = q_ref.shape[0]
    dk, dv = RET_DK, RET_DV

    def log_gamma(head, width):
        hv = jnp.full((1, width), head, jnp.int32).astype(F32)
        return jnp.log1p(-jnp.exp2(-5.0 - hv))

    def rows(n, width):
        return lax.broadcasted_iota(jnp.int32, (n, width), 0).astype(F32)

    @pl.when(c == 0)
    def _():
        for g in range(heads):
            head = hg * heads + g
            diff = rows(chunk, chunk) - lax.broadcasted_iota(jnp.int32, (chunk, chunk), 1).astype(F32)
            decay_sc[g] = jnp.where(diff >= 0, jnp.exp(log_gamma(head, chunk) * jnp.maximum(diff, 0.0)), 0.0)
            xi_sc[g] = jnp.exp(log_gamma(head, dv) * (rows(chunk, dv) + 1.0))
            zeta_sc[g] = jnp.exp(log_gamma(head, dk) * (chunk - 1.0 - rows(chunk, dk)))
            cd_sc[g] = jnp.exp(log_gamma(head, dv) * float(chunk))
            zeta_m = jnp.exp(log_gamma(head, dk) * (BLOCK - 1.0 - rows(BLOCK, dk)))
            kz = (km_ref[:, g * dk:(g + 1) * dk].astype(F32) * zeta_m).astype(BF16)
            state_sc[g] = _tn(kz, vm_ref[:, g * dv:(g + 1) * dv])

    for g in range(heads):
        q = q_ref[:, g * dk:(g + 1) * dk]
        k = k_ref[:, g * dk:(g + 1) * dk]
        v = v_ref[:, g * dv:(g + 1) * dv]
        s = _nt(q, k) * decay_sc[g]
        intra = jnp.dot(s.astype(BF16), v, preferred_element_type=F32)
        state = state_sc[g]
        cross = jnp.dot(q, state.astype(BF16), preferred_element_type=F32) * xi_sc[g]
        o = intra + cross
        kz = (k.astype(F32) * zeta_sc[g]).astype(BF16)
        state_sc[g] = state * cd_sc[g] + _tn(kz, v)
        mu = jnp.mean(o, axis=-1, keepdims=True)
        d = o - mu
        var = jnp.mean(d * d, axis=-1, keepdims=True)
        y = d * lax.rsqrt(var + EPS) * gn_ref[:, g * dv:(g + 1) * dv]
        o_ref[:, g * dv:(g + 1) * dv] = (g_ref[:, g * dv:(g + 1) * dv].astype(F32) * y).astype(o_ref.dtype)


def _retention(z3, zm, ret_norm_g, *, chunk, heads):
    b, s, _ = z3.shape
    nc = s // chunk
    wk, wv = heads * RET_DK, heads * RET_DV
    rq, rk = Z_RQ // wk, Z_RK // wk
    rv, gr = Z_RV // wv, Z_GR // wv
    return pl.pallas_call(
        functools.partial(_ret_kernel, heads=heads),
        grid=(b, RET_HEADS // heads, nc),
        in_specs=[
            pl.BlockSpec((None, chunk, wk), lambda b_, h, c: (b_, c, rq + h)),
            pl.BlockSpec((None, chunk, wk), lambda b_, h, c: (b_, c, rk + h)),
            pl.BlockSpec((None, chunk, wv), lambda b_, h, c: (b_, c, rv + h)),
            pl.BlockSpec((None, chunk, wv), lambda b_, h, c: (b_, c, gr + h)),
            pl.BlockSpec((BLOCK, wk), lambda b_, h, c: (0, rk + h)),
            pl.BlockSpec((BLOCK, wv), lambda b_, h, c: (0, rv + h)),
            pl.BlockSpec((1, wv), lambda b_, h, c: (0, h)),
        ],
        out_specs=pl.BlockSpec((None, chunk, wv), lambda b_, h, c: (b_, c, h)),
        out_shape=jax.ShapeDtypeStruct((b, s, RET_V_W), BF16),
        scratch_shapes=[pltpu.VMEM((heads, RET_DK, RET_DV), F32),
                        pltpu.VMEM((heads, chunk, chunk), F32),
                        pltpu.VMEM((heads, chunk, RET_DV), F32),
                        pltpu.VMEM((heads, chunk, RET_DK), F32),
                        pltpu.VMEM((heads, 1, RET_DV), F32)],
        compiler_params=_params(("parallel", "parallel", "arbitrary")),
        name="retention",
    )(z3, z3, z3, z3, zm, zm, ret_norm_g.reshape(1, RET_V_W))


def _merge_kernel(ya_ref, yr_ref, wf_ref, wr_ref, ga_ref, gr_ref, o_ref):
    a = jnp.dot(ya_ref[...], wf_ref[...], preferred_element_type=F32)
    r = jnp.dot(yr_ref[...], wr_ref[...], preferred_element_type=F32)
    o_ref[...] = (ga_ref[...].astype(F32) * a + gr_ref[...].astype(F32) * r).astype(o_ref.dtype)


def _merge(ya, yr, wf, wr, z, *, tm, tn, d_model):
    m = ya.shape[0]
    ga0 = Z_GA // tn
    gr0 = (Z_GA + d_model) // tn
    return pl.pallas_call(
        _merge_kernel,
        grid=(m // tm, d_model // tn),
        in_specs=[pl.BlockSpec((tm, FOX_W), lambda i, j: (i, 0)),
                  pl.BlockSpec((tm, RET_V_W), lambda i, j: (i, 0)),
                  pl.BlockSpec((FOX_W, tn), lambda i, j: (0, j)),
                  pl.BlockSpec((RET_V_W, tn), lambda i, j: (0, j)),
                  pl.BlockSpec((tm, tn), lambda i, j: (i, ga0 + j)),
                  pl.BlockSpec((tm, tn), lambda i, j: (i, gr0 + j))],
        out_specs=pl.BlockSpec((tm, tn), lambda i, j: (i, j)),
        out_shape=jax.ShapeDtypeStruct((m, d_model), BF16),
        compiler_params=_params(("parallel", "arbitrary"), V7X_VMEM_BUDGET),
        name="merge_proj",
    )(ya, yr, wf, wr, z, z)


def _outproj_kernel(a_ref, w_ref, x_ref, o_ref):
    o_ref[...] = x_ref[...] + jnp.dot(a_ref[...], w_ref[...], preferred_element_type=F32)


def _outproj(a, w, x, *, tm, tn):
    m, k = a.shape
    n = w.shape[1]
    return pl.pallas_call(
        _outproj_kernel,
        grid=(m // tm, n // tn),
        in_specs=[pl.BlockSpec((tm, k), lambda i, j: (i, 0)),
                  pl.BlockSpec((k, tn), lambda i, j: (0, j)),
                  pl.BlockSpec((tm, tn), lambda i, j: (i, j))],
        out_specs=pl.BlockSpec((tm, tn), lambda i, j: (i, j)),
        out_shape=jax.ShapeDtypeStruct((m, n), F32),
        compiler_params=_params(("parallel", "arbitrary"), V7X_VMEM_BUDGET),
        name="out_proj",
    )(a, w, x)


def _peer_score_kernel(x_ref, w_ref, keys_ref, o_ref):
    acc = jnp.dot(x_ref[...], w_ref[...], preferred_element_type=F32)
    for g in range(acc.shape[1] // PEER_HALF):
        qg = acc[:, g * PEER_HALF:(g + 1) * PEER_HALF].astype(BF16)
        o_ref[g] = _nt(keys_ref[g], qg)


def _peer_scores(xn, wq, keys, *, tm, tn):
    m, d = xn.shape
    n = wq.shape[1]
    ng = tn // PEER_HALF
    return pl.pallas_call(
        _peer_score_kernel,
        grid=(m // tm, n // tn),
        in_specs=[pl.BlockSpec((tm, d), lambda i, j: (i, 0)),
                  pl.BlockSpec((d, tn), lambda i, j: (0, j)),
                  pl.BlockSpec((ng, PEER_NKEYS, PEER_HALF), lambda i, j: (j, 0, 0))],
        out_specs=pl.BlockSpec((ng, PEER_NKEYS, tm), lambda i, j: (j, 0, i)),
        out_shape=jax.ShapeDtypeStruct((n // PEER_HALF, PEER_NKEYS, m), F32),
        compiler_params=_params(("parallel", "arbitrary"), V7X_VMEM_BUDGET),
        name="peer_scores",
    )(xn, wq, keys)


def _top_values(x, n, stacked_rows):
    rows, width = x.shape
    idx = lax.broadcasted_iota(jnp.int32, (rows, width), 0).astype(F32)
    srow = lax.broadcasted_iota(jnp.int32, (stacked_rows, width), 0)
    stacked = jnp.zeros((stacked_rows, width), F32)
    rank = jnp.full((rows, width), float(n), F32)
    vals = []
    for t in range(n):
        cur = jnp.max(x, axis=0, keepdims=True)
        vals.append(cur)
        stacked = jnp.where(srow == t, cur, stacked)
        first = jnp.min(jnp.where(x == cur, idx, float(rows)), axis=0, keepdims=True)
        hit = idx == first
        rank = jnp.where(hit, float(t), rank)
        x = jnp.where(hit, -jnp.inf, x)
    return stacked, vals, rank


def _peer_topk_kernel(s_ref, e1_ref, e2_ref, rank_ref, cnt_ref):
    s1 = s_ref[0]
    s2 = s_ref[1]
    v1, v1l, rank1 = _top_values(s1, PEER_TOPK, PEER_TOPK)
    v2, v2l, _ = _top_values(s2, PEER_TOPK, PEER_TOPK)
    sub = lax.broadcasted_iota(jnp.int32, (8, s1.shape[1]), 0)
    pieces = [v1l[0] + v2]
    for a in range(1, 8):
        nb = PEER_TOPK // (a + 1)
        pieces.append(jnp.where(sub < nb, v1l[a] + v2[0:8], -jnp.inf))
    pieces.append(v1[8:16] + v2l[0])
    cand = jnp.concatenate(pieces, axis=0)
    _, top, _ = _top_values(cand, PEER_TOPK, 8)
    m = top[0]
    z = jnp.zeros_like(m)
    for t in range(PEER_TOPK):
        z = z + jnp.exp(top[t] - m)
    tau = top[PEER_TOPK - 1]
    cnt = jnp.zeros_like(s2)
    for a in range(PEER_TOPK):
        cnt = cnt + jnp.where(v1l[a] + s2 >= tau, 1.0, 0.0)
    rank_ref[...] = rank1
    cnt_ref[...] = cnt.astype(cnt_ref.dtype)
    e1_ref[...] = jnp.exp(s1 - v1l[0])
    e2_ref[...] = (jnp.exp(s2 - v2l[0]) / z).astype(e2_ref.dtype)


def _peer_topk(st, *, tt):
    _, _, t = st.shape
    spec = pl.BlockSpec((None, PEER_NKEYS, tt), lambda i, h: (h, 0, i))
    shape = lambda dt: jax.ShapeDtypeStruct((PEER_HEADS, PEER_NKEYS, t), dt)
    return pl.pallas_call(
        _peer_topk_kernel,
        grid=(t // tt, PEER_HEADS),
        in_specs=[pl.BlockSpec((2, PEER_NKEYS, tt), lambda i, h: (h, 0, i))],
        out_specs=[spec, spec, spec, spec],
        out_shape=[shape(F32), shape(BF16), shape(F32), shape(BF16)],
        compiler_params=_params(("parallel", "parallel")),
        name="peer_topk",
    )(st)


def _gelu_exact(x):
    return 0.5 * x * (1.0 + lax.erf(x * (2.0 ** -0.5)))


def _peer_dense_kernel(xt_ref, u_ref, v_ref, e1_ref, rank_ref, e2_ref, cnt_ref, h_ref, o_ref, gate_sc, *,
                       na, halves):
    j = pl.program_id(1)

    def build_gate(chunk, gate_sc):
        sub = 16
        bshape = (sub, V7X_LANES)
        for a in range(na):
            row = chunk * na + a
            rkrows = [rank_ref[h, pl.ds(row, 1), :] for h in range(PEER_HEADS)]
            e1rows = [e1_ref[h, pl.ds(row, 1), :] for h in range(PEER_HEADS)]
            for l0 in range(0, gate_sc.shape[1], V7X_LANES):
                ls = slice(l0, l0 + V7X_LANES)
                rkb = [jnp.broadcast_to(rkrows[h][:, ls], bshape).astype(BF16) for h in range(PEER_HEADS)]
                e1b = [jnp.broadcast_to(e1rows[h][:, ls], bshape).astype(BF16) for h in range(PEER_HEADS)]
                for r in range(0, PEER_NKEYS, sub):
                    gate = None
                    for h in range(PEER_HEADS):
                        sel = rkb[h] < cnt_ref[h, r:r + sub, ls]
                        term = jnp.where(sel, e1b[h] * e2_ref[h, r:r + sub, ls], jnp.zeros(bshape, BF16))
                        gate = term if gate is None else gate + term
                    gate_sc[a * PEER_NKEYS + r:a * PEER_NKEYS + r + sub, ls] = gate

    @pl.when(j == 0)
    def _():
        rows = o_ref.shape[0]
        pltpu.sync_copy(h_ref.at[pl.ds(pl.program_id(0) * rows, rows), :], o_ref)

    build_gate(j, gate_sc)
    rows = na * PEER_NKEYS // halves
    total = None
    for k in range(halves):
        sl = slice(k * rows, (k + 1) * rows)
        act = jnp.dot(u_ref[sl, :], xt_ref[...], preferred_element_type=F32)
        w = gate_sc[sl, :] * _gelu_exact(act).astype(BF16)
        part = _tn(w, v_ref[sl, :])
        total = part if total is None else total + part
    o_ref[...] += total


def _peer_dense(xnt, u_tab, v_tab, e1, rank1, e2, cnt, h, *, tt, ec):
    d, t = xnt.shape
    n_exp = u_tab.shape[0]
    na = ec // PEER_NKEYS
    once = pl.Buffered(1)
    kern = functools.partial(_peer_dense_kernel, na=na, halves=2)
    table = pl.BlockSpec((PEER_HEADS, PEER_NKEYS, tt), lambda i, j: (0, 0, i), pipeline_mode=once)
    return pl.pallas_call(
        kern,
        grid=(t // tt, n_exp // ec),
        in_specs=[pl.BlockSpec((d, tt), lambda i, j: (0, i), pipeline_mode=once),
                  pl.BlockSpec((ec, d), lambda i, j: (j, 0)),
                  pl.BlockSpec((ec, d), lambda i, j: (j, 0)),
                  table, table, table, table,
                  pl.BlockSpec(memory_space=pl.ANY)],
        out_specs=pl.BlockSpec((tt, d), lambda i, j: (i, 0), pipeline_mode=once),
        out_shape=jax.ShapeDtypeStruct((t, d), F32),
        scratch_shapes=[pltpu.VMEM((ec, tt), BF16)],
        compiler_params=_params(("parallel", "arbitrary"), V7X_VMEM_BYTES - 4 * 1024 * 1024),
        name="peer_dense",
    )(xnt, u_tab, v_tab, e1, rank1, e2, cnt, h)


def _rotary_tables(pos, valid):
    half = RET_DK // 2
    inv = ROPE_BASE ** (-jnp.arange(half, dtype=F32) / half)
    ang = pos[:, None] * inv[None, :]
    cos, sin = jnp.cos(ang), jnp.sin(ang)
    c = jnp.concatenate([cos, cos], axis=-1)
    s = jnp.concatenate([-sin, sin], axis=-1)
    kscale = (RET_DK ** -0.5) * valid.astype(F32)[:, None]
    return jnp.stack([c, c * kscale]), jnp.stack([s, s * kscale])


def _tile(n, pref):
    t = min(n, pref)
    assert n % t == 0
    return t


def kernel(x, meta_tokens, norm_mix_g, w_in, b_forget, q_norm_g, k_norm_g, ret_norm_g, w_proj_fox,
           w_proj_ret, w_out, norm_ffn_g, peer_w_q, peer_keys_1, peer_keys_2, peer_u, peer_v):
    b, s, d = x.shape
    depth = w_in.shape[0]
    assert depth == 1, "only the real-token outputs of a single layer are computed"
    m = b * s
    xr = x.reshape(m, d)
    hm = jnp.concatenate([jnp.zeros((PAD, d), x.dtype), meta_tokens.astype(x.dtype)], axis=0)

    tm = _tile(m, 1024)
    tn = _tile(d, 1024)
    wt = jnp.swapaxes(w_in[0], 0, 1).astype(BF16)
    o_f = 3 * FOX_W
    o_r = o_f + FOX_HEADS
    groups = [(0, 2 * FOX_W), (o_r, 2 * RET_QK_W), (2 * FOX_W, FOX_W),
              (o_r + 2 * RET_QK_W, 2 * RET_V_W + 2 * d)]
    src_rows = jnp.asarray([start + t * tn for start, width in groups for t in range(width // tn)],
                           jnp.int32)
    npk = src_rows.shape[0] * tn
    wft = jnp.pad(wt[o_f:o_f + FOX_HEADS], ((0, V7X_LANES - FOX_HEADS), (0, 0)))
    bf = jnp.pad(b_forget[0], (0, V7X_LANES - FOX_HEADS)).reshape(1, V7X_LANES)
    gain = jnp.concatenate([jnp.tile(q_norm_g[0] * (FOX_HD ** -0.5 * LOG2_E), FOX_HEADS),
                            jnp.tile(k_norm_g[0], FOX_HEADS),
                            jnp.zeros((npk - 2 * FOX_W,), F32)]).reshape(1, npk)

    pos_real = jnp.arange(s, dtype=F32) + N_META
    rc, rs = _rotary_tables(pos_real, jnp.ones((s,), bool))
    pos_meta = jnp.arange(BLOCK, dtype=F32) - PAD
    rcm, rsm = _rotary_tables(pos_meta, jnp.arange(BLOCK) >= PAD)

    um, cm, _, ckxm = _forget_cumsum(hm, norm_mix_g[0], wft, bf, jnp.zeros((1, V7X_LANES), F32), nb=1,
                                     tm=BLOCK, first_valid_row=PAD)
    u, _, cqx, ckx = _forget_cumsum(xr, norm_mix_g[0], wft, bf, cm[BLOCK - 1:BLOCK], nb=b,
                                    tm=_tile(s, 512), first_valid_row=0)
    z = _proj(u, wt, src_rows, gain, rc, rs, tm=_tile(s, 1024), tn=tn, d_model=d, first_valid_row=0)
    zm = _proj(um, wt, src_rows, gain, rcm, rsm, tm=BLOCK, tn=tn, d_model=d, first_valid_row=PAD)

    z3 = z.reshape(b, s, npk)
    ya = _fox_attention(z3, zm, cqx.reshape(b, s, V7X_LANES), ckx.reshape(b, s, FOX_W), ckxm,
                        tq=_tile(s, 512), heads=16)
    chunk = _tile(s, 256)
    yr = _retention(z3, zm, ret_norm_g[0], chunk=chunk, heads=8)

    merged = _merge(ya.reshape(m, FOX_W), yr.reshape(m, RET_V_W), w_proj_fox[0].astype(BF16),
                    w_proj_ret[0].astype(BF16), z, tm=tm, tn=_tile(d, 512), d_model=d)
    h1 = _outproj(merged, w_out[0].astype(BF16), xr, tm=tm, tn=tn)

    xn, xnt = _rmsnorm_bf16(h1, norm_ffn_g[0], _tile(m, 512), with_transpose=True)
    keys = jnp.stack([peer_keys_1[0], peer_keys_2[0]], axis=1).reshape(
        2 * PEER_HEADS, PEER_NKEYS, PEER_HALF).astype(BF16)
    st = _peer_scores(xn, peer_w_q[0].astype(BF16), keys, tm=tm, tn=_tile(PEER_HEADS * PEER_QDIM, 1024))
    e1, e2, rank1, cnt = _peer_topk(st, tt=_tile(m, 512))
    out = _peer_dense(xnt, peer_u[0].astype(BF16), peer_v[0].astype(BF16), e1, rank1, e2, cnt, h1,
                      tt=_tile(m, 512), ec=1024)
    return out.reshape(b, s, d)
```
